```python
import math
import jax
import jax.numpy as jnp
from jax import lax
import numpy as np

D_MODEL = 1024
BATCH = 1
SEQ = 16384
DEPTH = 2
DEC_BATCH = 32
DEC_SEQ = 4
PAST_LEN = 16384
PAGE_SIZE = 128

D_MIX = D_MODEL
ATTN_WIDTH = D_MIX // 2
CONV_WIDTH = D_MIX - ATTN_WIDTH
HEAD_DIM = 64
V_DIM = 2 * HEAD_DIM
N_HEADS = ATTN_WIDTH // V_DIM
ROT_DIM = HEAD_DIM // 4
ROPE_THETA = 500000.0
CONV_K = 3
D_FF = 2816
D_PLE = 256
Q_BLOCK = 128
RMS_EPS = 1e-6
POOL_NUM = 5
POOL_DEN = 4
D_IN = 3 * ATTN_WIDTH + 3 * CONV_WIDTH

kernel_name = "hymba_diffattn_shortconv_convffn_decode_step"


def rms_norm(x, g):
    xf = x.astype(jnp.float32)
    y = xf * lax.rsqrt(jnp.mean(xf * xf, axis=-1, keepdims=True) + RMS_EPS)
    return (y * g.astype(jnp.float32)).astype(x.dtype)


def partial_rope(x, pos):
    half = ROT_DIM // 2
    inv = ROPE_THETA ** (-jnp.arange(0, ROT_DIM, 2, dtype=jnp.float32) / ROT_DIM)
    ang = pos.astype(jnp.float32)[:, None] * inv[None, :]
    cos = jnp.cos(ang)[None, :, None, None, :]
    sin = jnp.sin(ang)[None, :, None, None, :]
    xr = x[..., :ROT_DIM].astype(jnp.float32)
    x1, x2 = xr[..., :half], xr[..., half:]
    rot = jnp.concatenate([x1 * cos - x2 * sin, x2 * cos + x1 * sin], axis=-1)
    return jnp.concatenate([rot.astype(x.dtype), x[..., ROT_DIM:]], axis=-1)


def causal_dwconv(u, prev, w):
    t = u.shape[1]
    full = jnp.concatenate([prev.astype(u.dtype), u], axis=1)
    y = w[0] * full[:, 0:t]
    for j in range(1, CONV_K):
        y = y + w[j] * full[:, j:j + t]
    return y, full[:, -(CONV_K - 1):]


def diff_lambda(lam_qk, lam_init):
    l = lam_qk.astype(jnp.float32)
    return jnp.exp(jnp.sum(l[0] * l[1])) - jnp.exp(jnp.sum(l[2] * l[3])) + lam_init


def diff_attention(q, k, v, q_pos, k_pos, lam):
    s = jnp.einsum('bqhmd,bkhmd->bhmqk', q, k).astype(jnp.float32) * (HEAD_DIM ** -0.5)
    mask = k_pos[None, :] <= q_pos[:, None]
    s = jnp.where(mask, s, -jnp.inf)
    pr = jax.nn.softmax(s, axis=-1)
    wgt = pr[:, :, 0] - lam * pr[:, :, 1]
    return jnp.einsum('bhqk,bkhe->bqhe', wgt.astype(v.dtype), v)


def blocked_causal_diff_attention(q, k, v, lam):
    b, s = q.shape[:2]
    nb = s // Q_BLOCK
    pos = jnp.arange(s)
    qb = q.reshape((b, nb, Q_BLOCK) + q.shape[2:]).swapaxes(0, 1)
    pb = pos.reshape(nb, Q_BLOCK)
    out = lax.map(lambda a: diff_attention(a[0], k, v, a[1], pos, lam), (qb, pb))
    return out.swapaxes(0, 1).reshape(b, s, N_HEADS, V_DIM)


def run_trunk(x, p, q_pos, conv_prev, ffn_prev, attend, params):
    (g_attn, w_in, lambda_qk, g_subln, conv_w, w_o, g_ffn, w_up,
     ffn_conv_w, w_down, w_ple, w_pg, g_final) = params
    b, t = x.shape[:2]
    splits = [ATTN_WIDTH, 2 * ATTN_WIDTH, 3 * ATTN_WIDTH,
              3 * ATTN_WIDTH + CONV_WIDTH, 3 * ATTN_WIDTH + 2 * CONV_WIDTH]
    h = x
    ks, vs, cs, fs = [], [], [], []
    for i in range(DEPTH):
        lam_init = 0.8 - 0.6 * math.exp(-0.3 * i)
        a = rms_norm(h, g_attn[i])
        z = a @ w_in[i]
        q, k, v, gb, gc, u = jnp.split(z, splits, axis=-1)
        q = partial_rope(q.reshape(b, t, N_HEADS, 2, HEAD_DIM), q_pos)
        k = partial_rope(k.reshape(b, t, N_HEADS, 2, HEAD_DIM), q_pos)
        v = v.reshape(b, t, N_HEADS, V_DIM)
        lam = diff_lambda(lambda_qk[i], lam_init)
        o = attend(i, q, k, v, lam)
        attn_out = (rms_norm(o, g_subln[i]) * (1.0 - lam_init)).reshape(b, t, ATTN_WIDTH)
        cy, c_state = causal_dwconv(gc * u, conv_prev[i], conv_w[i])
        conv_out = gb * cy
        h = h + jnp.concatenate([attn_out, conv_out], axis=-1) @ w_o[i]
        f = rms_norm(h, g_ffn[i])
        gate, up = jnp.split(f @ w_up[i], 2, axis=-1)
        gate_c, f_state = causal_dwconv(gate, ffn_prev[i], ffn_conv_w[i])
        h = h + (jax.nn.silu(gate_c) * up) @ w_down[i]
        h = h + (p[i] @ w_ple[i]) * jax.nn.sigmoid(h @ w_pg[i])
        ks.append(k)
        vs.append(v)
        cs.append(c_state)
        fs.append(f_state)
    y = rms_norm(h, g_final)
    return y, jnp.stack(ks), jnp.stack(vs), jnp.stack(cs), jnp.stack(fs)


def setup_inputs(seed: int = 0) -> dict:
    key = jax.random.key(seed)
    ks = jax.random.split(key, 24)
    f32 = jnp.float32
    n_pages = PAST_LEN // PAGE_SIZE
    n_phys = (DEC_BATCH * n_pages * POOL_NUM + POOL_DEN - 1) // POOL_DEN

    def nrm(k, shape, scale=1.0):
        return jax.random.normal(k, shape, f32) * scale

    def gain(k, shape):
        return 1.0 + 0.02 * jax.random.normal(k, shape, f32)

    page_table = jax.random.permutation(ks[6], n_phys)[: DEC_BATCH * n_pages]
    page_table = page_table.reshape(DEC_BATCH, n_pages).astype(jnp.int32)
    return {
        "x_prompt": nrm(ks[0], (BATCH, SEQ, D_MODEL)),
        "x_sample": nrm(ks[1], (DEC_BATCH, DEC_SEQ, D_MODEL)),
        "cache_k": nrm(ks[2], (DEPTH, n_phys, PAGE_SIZE, N_HEADS, 2, HEAD_DIM)),
        "cache_v": nrm(ks[3], (DEPTH, n_phys, PAGE_SIZE, N_HEADS, V_DIM)),
        "state_conv": nrm(ks[4], (DEPTH, DEC_BATCH, CONV_K - 1, CONV_WIDTH)),
        "state_ffn_conv": nrm(ks[5], (DEPTH, DEC_BATCH, CONV_K - 1, D_FF)),
        "page_table": page_table,
        "p_prompt": nrm(ks[7], (DEPTH, BATCH, SEQ, D_PLE)),
        "p_sample": nrm(ks[8], (DEPTH, DEC_BATCH, DEC_SEQ, D_PLE)),
        "g_attn": gain(ks[9], (DEPTH, D_MODEL)),
        "w_in": nrm(ks[10], (DEPTH, D_MODEL, D_IN), D_MODEL ** -0.5),
        "lambda_qk": nrm(ks[11], (DEPTH, 4, HEAD_DIM), 0.1),
        "g_subln": gain(ks[12], (DEPTH, V_DIM)),
        "conv_w": nrm(ks[13], (DEPTH, CONV_K, CONV_WIDTH), CONV_K ** -0.5),
        "w_o": nrm(ks[14], (DEPTH, D_MIX, D_MODEL), D_MIX ** -0.5),
        "g_ffn": gain(ks[15], (DEPTH, D_MODEL)),
        "w_up": nrm(ks[16], (DEPTH, D_MODEL, 2 * D_FF), D_MODEL ** -0.5),
        "ffn_conv_w": nrm(ks[17], (DEPTH, CONV_K, D_FF), CONV_K ** -0.5),
        "w_down": nrm(ks[18], (DEPTH, D_FF, D_MODEL), D_FF ** -0.5),
        "w_ple": nrm(ks[19], (DEPTH, D_PLE, D_MODEL), D_PLE ** -0.5),
        "w_pg": nrm(ks[20], (DEPTH, D_MODEL, D_MODEL), D_MODEL ** -0.5),
        "g_final": gain(ks[21], (D_MODEL,)),
    }


def reference(x_prompt, x_sample, cache_k, cache_v, state_conv, state_ffn_conv, page_table,
              p_prompt, p_sample, g_attn, w_in, lambda_qk, g_subln, conv_w, w_o, g_ffn,
              w_up, ffn_conv_w, w_down, w_ple, w_pg, g_final):
    params = (g_attn, w_in, lambda_qk, g_subln, conv_w, w_o, g_ffn, w_up,
              ffn_conv_w, w_down, w_ple, w_pg, g_final)
    b, s = x_prompt.shape[:2]
    db, t = x_sample.shape[:2]
    past_len = page_table.shape[1] * cache_k.shape[2]

    def attend_prompt(i, q, k, v, lam):
        return blocked_causal_diff_attention(q, k, v, lam)

    def attend_sample(i, q, k, v, lam):
        past_k = cache_k[i, page_table].reshape((db, past_len) + cache_k.shape[3:]).astype(k.dtype)
        past_v = cache_v[i, page_table].reshape((db, past_len) + cache_v.shape[3:]).astype(v.dtype)
        keys = jnp.concatenate([past_k, k], axis=1)
        vals = jnp.concatenate([past_v, v], axis=1)
        k_pos = jnp.arange(past_len + t)
        q_pos = past_len + jnp.arange(t)
        return diff_attention(q, keys, vals, q_pos, k_pos, lam)

    conv0 = jnp.zeros((DEPTH, b, CONV_K - 1, CONV_WIDTH), x_prompt.dtype)
    ffn0 = jnp.zeros((DEPTH, b, CONV_K - 1, D_FF), x_prompt.dtype)
    y_prompt, k_p, v_p, c_p, f_p = run_trunk(
        x_prompt, p_prompt, jnp.arange(s), conv0, ffn0, attend_prompt, params)
    y_sample, k_s, v_s, c_s, f_s = run_trunk(
        x_sample, p_sample, past_len + jnp.arange(t), state_conv, state_ffn_conv,
        attend_sample, params)
    return (y_prompt, y_sample, k_p, v_p, c_p, f_p, k_s, v_s, c_s, f_s)
```

```python
import functools
import math

import jax
import jax.numpy as jnp
from jax import lax
from jax.experimental import pallas as pl
from jax.experimental.pallas import tpu as pltpu

HEAD_DIM = 64
HEAD_SHIFT = 6
V_DIM = 2 * HEAD_DIM
ROT_DIM = HEAD_DIM // 4
ROPE_THETA = 500000.0
CONV_K = 3
RMS_EPS = 1e-6
LANES = 128
SUBLANES = 8
VMEM_LIMIT = 56 * 1024 * 1024

F32 = jnp.float32
BF16 = jnp.bfloat16
NT_DIMS = (((1,), (1,)), ((), ()))


def _round_up(x, m):
    return (x + m - 1) // m * m


def _rms(x, g):
    return x * lax.rsqrt(jnp.mean(x * x, axis=-1, keepdims=True) + RMS_EPS) * g


def _resident(shape):
    return pl.BlockSpec(shape, lambda *_: (0,) * len(shape), pipeline_mode=pl.Buffered(1))


def _diff_lambda(lamqk, lam_init):
    a = jnp.sum(lamqk[0:1] * lamqk[1:2], axis=1, keepdims=True)
    b = jnp.sum(lamqk[2:3] * lamqk[3:4], axis=1, keepdims=True)
    return jnp.exp(a) - jnp.exp(b) + lam_init


def _causal_conv3(buf_ref, cols, w, x, pad, stride, tm):
    x2 = buf_ref[pad - 2 * stride:pad - 2 * stride + tm, cols]
    x1 = buf_ref[pad - stride:pad - stride + tm, cols]
    return w[0:1] * x2 + w[1:2] * x1 + w[2:3] * x


def _in_proj_kernel(h_ref, g_ref, w_ref, cos_ref, sa_ref, sb_ref, cw_ref, prev_ref,
                    q_ref, k_ref, kb_ref, v_ref, vb_ref, co_ref, st_ref, cbuf,
                    *, tm, pad, stride, aw, cwid):
    i = pl.program_id(0)
    a = _rms(h_ref[...], g_ref[...]).astype(BF16)

    def proj(lo, width):
        return jnp.dot(a, w_ref[:, lo:lo + width], preferred_element_type=F32)

    cos, sa, sb = cos_ref[...], sa_ref[...], sb_ref[...]
    half = ROT_DIM // 2

    def rope(x):
        outs = []
        for hh in range(aw // LANES):
            xs = x[:, hh * LANES:(hh + 1) * LANES]
            up = pltpu.roll(xs, LANES - half, axis=1)
            dn = pltpu.roll(xs, half, axis=1)
            outs.append(xs * cos + up * sa + dn * sb)
        return jnp.concatenate(outs, axis=1)

    q = rope(proj(0, aw))
    q_ref[...] = (q * (HEAD_DIM ** -0.5)).astype(BF16)
    k = rope(proj(aw, aw))
    k_ref[...] = k
    kb_ref[...] = k.astype(BF16)
    v = proj(2 * aw, aw)
    v_ref[...] = v
    vb_ref[...] = v.astype(BF16)

    gb = proj(3 * aw, cwid)
    cu = proj(3 * aw + cwid, cwid) * proj(3 * aw + 2 * cwid, cwid)

    @pl.when(i == 0)
    def _():
        cbuf[0:pad, :] = prev_ref[...]

    cbuf[pad:pad + tm, :] = cu
    y = _causal_conv3(cbuf, slice(None), cw_ref[...], cu, pad, stride, tm)
    co_ref[...] = (gb * y).astype(BF16)
    tail = cbuf[tm:tm + pad, :]
    st_ref[...] = tail
    cbuf[0:pad, :] = tail


def _in_proj(h, g, w_bf, tabs, conv_w, prev, *, tm, stride):
    r, d = h.shape
    aw = cwid = conv_w.shape[1]
    pad = prev.shape[0]
    cos, sa, sb = tabs
    row = lambda width: pl.BlockSpec((tm, width), lambda i: (i, 0))
    kern = functools.partial(_in_proj_kernel, tm=tm, pad=pad, stride=stride, aw=aw, cwid=cwid)
    return pl.pallas_call(
        kern,
        grid=(r // tm,),
        in_specs=[row(d), _resident((1, d)), _resident(w_bf.shape),
                  row(LANES), row(LANES), row(LANES),
                  _resident(conv_w.shape), _resident(prev.shape)],
        out_specs=[row(aw), row(aw), row(aw), row(aw), row(aw), row(cwid),
                   pl.BlockSpec((pad, cwid), lambda i: (0, 0))],
        out_shape=[jax.ShapeDtypeStruct((r, aw), BF16),
                   jax.ShapeDtypeStruct((r, aw), F32),
                   jax.ShapeDtypeStruct((r, aw), BF16),
                   jax.ShapeDtypeStruct((r, aw), F32),
                   jax.ShapeDtypeStruct((r, aw), BF16),
                   jax.ShapeDtypeStruct((r, cwid), BF16),
                   jax.ShapeDtypeStruct((pad, cwid), F32)],
        scratch_shapes=[pltpu.VMEM((pad + tm, cwid), F32)],
        compiler_params=pltpu.CompilerParams(dimension_semantics=("arbitrary",),
                                             vmem_limit_bytes=VMEM_LIMIT),
        name="in_proj",
    )(h, g, w_bf, cos, sa, sb, conv_w, prev)


def _flash_kernel(q_ref, k_ref, v_ref, lam_ref, g_ref, o_ref, *, tq, lam_init):
    i = pl.program_id(1)
    q = q_ref[...]
    lane = lax.broadcasted_iota(jnp.int32, q.shape, 1)
    zero = jnp.zeros_like(q)
    qz = jnp.concatenate([jnp.where(lane < HEAD_DIM, q, zero),
                          jnp.where(lane >= HEAD_DIM, q, zero)], axis=0)

    def step(j, carry, masked):
        m, l, acc = carry
        start = pl.multiple_of(j * tq, tq)
        kt = k_ref[pl.ds(start, tq), :]
        vt = v_ref[pl.ds(start, tq), :]
        s = lax.dot_general(qz, kt, NT_DIMS, preferred_element_type=F32)
        if masked:
            qi = lax.broadcasted_iota(jnp.int32, (2 * tq, tq), 0)
            qi = jnp.where(qi >= tq, qi - tq, qi)
            ki = lax.broadcasted_iota(jnp.int32, (2 * tq, tq), 1)
            s = jnp.where(ki <= qi, s, -jnp.inf)
        m_new = jnp.maximum(m, jnp.max(s, axis=1, keepdims=True))
        alpha = jnp.exp(m - m_new)
        p = jnp.exp(s - m_new)
        l = alpha * l + jnp.sum(p, axis=1, keepdims=True)
        acc = alpha * acc + jnp.dot(p.astype(BF16), vt, preferred_element_type=F32)
        return m_new, l, acc

    init = (jnp.full((2 * tq, 1), -jnp.inf, F32), jnp.zeros((2 * tq, 1), F32),
            jnp.zeros((2 * tq, V_DIM), F32))
    carry = lax.fori_loop(0, i, lambda j, c: step(j, c, False), init)
    _, l, acc = step(i, carry, True)

    o = acc / l
    lam = _diff_lambda(lam_ref[...], lam_init)
    o = o[0:tq] - lam * o[tq:2 * tq]
    o_ref[...] = (_rms(o, g_ref[...]) * (1.0 - lam_init)).astype(o_ref.dtype)


def _prompt_attention(q, k, v, lamqk, g_sub, *, tq, lam_init):
    t, aw = q.shape
    nh = aw // V_DIM
    return pl.pallas_call(
        functools.partial(_flash_kernel, tq=tq, lam_init=lam_init),
        grid=(nh, t // tq),
        in_specs=[pl.BlockSpec((tq, V_DIM), lambda h, i: (i, h)),
                  pl.BlockSpec((t, V_DIM), lambda h, i: (0, h)),
                  pl.BlockSpec((t, V_DIM), lambda h, i: (0, h)),
                  _resident(lamqk.shape), _resident(g_sub.shape)],
        out_specs=pl.BlockSpec((tq, V_DIM), lambda h, i: (i, h)),
        out_shape=jax.ShapeDtypeStruct((t, aw), BF16),
        compiler_params=pltpu.CompilerParams(dimension_semantics=("arbitrary", "arbitrary"),
                                             vmem_limit_bytes=VMEM_LIMIT),
        name="prompt_attention",
    )(q, k, v, lamqk, g_sub)


def _decode_kernel(pt_ref, q_ref, kn_ref, vn_ref, lam_ref, g_ref, *rest, npg, nt, nh, lam_init):
    kt_refs = rest[:npg]
    v_refs = rest[npg:2 * npg]
    o_ref, qbd_ref, m_ref, l_ref, acc_ref = rest[2 * npg:]
    c = pl.program_id(1)
    nrow, aw = qbd_ref.shape
    hrows = nrow // nh

    @pl.when(c == 0)
    def _():
        qrep = q_ref[0]
        r = lax.broadcasted_iota(jnp.int32, (nrow, aw), 0)
        lo = (lax.broadcasted_iota(jnp.int32, (nrow, aw), 1) >> HEAD_SHIFT) * nt
        qbd_ref[...] = jnp.where((r >= lo) & (r < lo + nt), qrep, 0.0).astype(BF16)
        m_ref[...] = jnp.full(m_ref.shape, -jnp.inf, F32)
        l_ref[...] = jnp.zeros(l_ref.shape, F32)
        acc_ref[...] = jnp.zeros(acc_ref.shape, F32)

    qbd = qbd_ref[...]
    s = jnp.concatenate(
        [jnp.dot(qbd, kt[...].astype(BF16), preferred_element_type=F32) for kt in kt_refs],
        axis=1)
    m_prev = m_ref[...]
    m_new = jnp.maximum(m_prev, jnp.max(s, axis=1, keepdims=True))
    alpha = jnp.exp(m_prev - m_new)
    p = jnp.exp(s - m_new)
    l_ref[...] = alpha * l_ref[...] + jnp.sum(p, axis=1, keepdims=True)
    page = s.shape[1] // npg
    pv = []
    for hh in range(nh):
        ph = p[hh * hrows:(hh + 1) * hrows].astype(BF16)
        pvh = jnp.zeros((hrows, V_DIM), F32)
        for j in range(npg):
            pvh += jnp.dot(ph[:, j * page:(j + 1) * page], v_refs[j][:, hh, :].astype(BF16),
                           preferred_element_type=F32)
        pv.append(pvh)
    acc_ref[...] = alpha * acc_ref[...] + jnp.concatenate(pv, axis=0)
    m_ref[...] = m_new

    @pl.when(c == pl.num_programs(1) - 1)
    def _():
        qf = qbd.astype(F32)
        kn, vn = kn_ref[0], vn_ref[0]
        tq = lax.broadcasted_iota(jnp.int32, (nrow, 1), 0) & (nt - 1)
        s_new = []
        for tk in range(nt):
            st = jnp.sum(qf * kn[tk:tk + 1, :], axis=1, keepdims=True)
            s_new.append(jnp.where(tk <= tq, st, -jnp.inf))
        m_fin = m_new
        for st in s_new:
            m_fin = jnp.maximum(m_fin, st)
        a_fin = jnp.exp(m_new - m_fin)
        l = a_fin * l_ref[...]
        acc = a_fin * acc_ref[...]
        for tk in range(nt):
            pn = jnp.exp(s_new[tk] - m_fin)
            l = l + pn
            v_rows = jnp.concatenate(
                [jnp.broadcast_to(vn[tk:tk + 1, hh * V_DIM:(hh + 1) * V_DIM], (hrows, V_DIM))
                 for hh in range(nh)], axis=0)
            acc = acc + pn * v_rows
        acc_ref[...] = acc / l
        lam = _diff_lambda(lam_ref[...], lam_init)
        g = g_ref[...]
        for hh in range(nh):
            r0 = hrows * hh
            o = acc_ref[r0:r0 + nt, :] - lam * acc_ref[r0 + nt:r0 + 2 * nt, :]
            o_ref[0, :, hh * V_DIM:(hh + 1) * V_DIM] = _rms(o, g) * (1.0 - lam_init)


def _sample_attention(page_table, q_rep, k_new, v_new, lamqk, g_sub, cache_kt, cache_v,
                      *, layer, npg, lam_init):
    nb, nt, aw = k_new.shape
    n_pages = page_table.shape[1]
    page, nh = cache_v.shape[2], cache_v.shape[3]
    nrow = q_rep.shape[1]
    pt_flat = page_table.reshape(-1)

    def phys(b, c, pt, j):
        return pt[b * n_pages + c * npg + j]

    def kt_spec(j):
        return pl.BlockSpec((None, None, aw, page), lambda b, c, pt: (layer, phys(b, c, pt, j), 0, 0))

    def v_spec(j):
        return pl.BlockSpec((None, None, page, nh, V_DIM),
                            lambda b, c, pt: (layer, phys(b, c, pt, j), 0, 0, 0))

    per_seq = lambda rows: pl.BlockSpec((1, rows, aw), lambda b, c, pt: (b, 0, 0))
    const = lambda shape: pl.BlockSpec(shape, lambda b, c, pt: (0,) * len(shape))
    grid_spec = pltpu.PrefetchScalarGridSpec(
        num_scalar_prefetch=1,
        grid=(nb, n_pages // npg),
        in_specs=[per_seq(nrow), per_seq(nt), per_seq(nt), const(lamqk.shape), const(g_sub.shape)]
                 + [kt_spec(j) for j in range(npg)] + [v_spec(j) for j in range(npg)],
        out_specs=per_seq(nt),
        scratch_shapes=[pltpu.VMEM((nrow, aw), BF16), pltpu.VMEM((nrow, 1), F32),
                        pltpu.VMEM((nrow, 1), F32), pltpu.VMEM((nrow, V_DIM), F32)],
    )
    return pl.pallas_call(
        functools.partial(_decode_kernel, npg=npg, nt=nt, nh=nh, lam_init=lam_init),
        grid_spec=grid_spec,
        out_shape=jax.ShapeDtypeStruct((nb, nt, aw), F32),
        compiler_params=pltpu.CompilerParams(dimension_semantics=("arbitrary", "arbitrary"),
                                             vmem_limit_bytes=VMEM_LIMIT),
        name="sample_attention",
    )(pt_flat, q_rep, k_new, v_new, lamqk, g_sub, *([cache_kt] * npg), *([cache_v] * npg))


def _post_kernel(h_ref, at_ref, co_ref, wo_ref, gf_ref, wup_ref, fcw_ref, prev_ref, wd_ref,
                 p_ref, wple_ref, wpg_ref, gfin_ref, out_ref, st_ref, gbuf,
                 *, tm, pad, stride, dff, chunk, final):
    i = pl.program_id(0)
    cat = jnp.concatenate([at_ref[...], co_ref[...]], axis=1)
    h1 = h_ref[...] + jnp.dot(cat, wo_ref[...], preferred_element_type=F32)
    f = _rms(h1, gf_ref[...]).astype(BF16)

    @pl.when(i == 0)
    def _():
        gbuf[0:pad, :] = prev_ref[...]

    acc = h1
    for lo in range(0, dff, chunk):
        cols = slice(lo, lo + chunk)
        gate = jnp.dot(f, wup_ref[:, lo:lo + chunk], preferred_element_type=F32)
        up = jnp.dot(f, wup_ref[:, dff + lo:dff + lo + chunk], preferred_element_type=F32)
        gbuf[pad:pad + tm, cols] = gate
        gate_c = _causal_conv3(gbuf, cols, fcw_ref[:, cols], gate, pad, stride, tm)
        act = (gate_c * jax.nn.sigmoid(gate_c) * up).astype(BF16)
        acc = acc + jnp.dot(act, wd_ref[lo:lo + chunk, :], preferred_element_type=F32)

    tail = gbuf[tm:tm + pad, :]
    st_ref[...] = tail
    gbuf[0:pad, :] = tail

    h2 = acc
    ple = jnp.dot(p_ref[...].astype(BF16), wple_ref[...], preferred_element_type=F32)
    pg = jnp.dot(h2.astype(BF16), wpg_ref[...], preferred_element_type=F32)
    h3 = h2 + ple * jax.nn.sigmoid(pg)
    out_ref[...] = _rms(h3, gfin_ref[...]) if final else h3


def _post(h, attn, conv, wo_bf, g_ffn, wup_bf, ffn_conv_w, prev, wd_bf, p, wple_bf, wpg_bf, g_final,
          *, tm, stride, chunk, final):
    r, d = h.shape
    dff = wd_bf.shape[0]
    pad = prev.shape[0]
    row = lambda width: pl.BlockSpec((tm, width), lambda i: (i, 0))
    kern = functools.partial(_post_kernel, tm=tm, pad=pad, stride=stride, dff=dff, chunk=chunk,
                             final=final)
    return pl.pallas_call(
        kern,
        grid=(r // tm,),
        in_specs=[row(d), row(attn.shape[1]), row(conv.shape[1]), _resident(wo_bf.shape),
                  _resident((1, d)), _resident(wup_bf.shape), _resident(ffn_conv_w.shape),
                  _resident(prev.shape), _resident(wd_bf.shape), row(p.shape[1]),
                  _resident(wple_bf.shape), _resident(wpg_bf.shape), _resident((1, d))],
        out_specs=[row(d), pl.BlockSpec((pad, dff), lambda i: (0, 0))],
        out_shape=[jax.ShapeDtypeStruct((r, d), F32),
                   jax.ShapeDtypeStruct((pad, dff), F32)],
        scratch_shapes=[pltpu.VMEM((pad + tm, dff), F32)],
        compiler_params=pltpu.CompilerParams(dimension_semantics=("arbitrary",),
                                             vmem_limit_bytes=VMEM_LIMIT),
        name="post",
    )(h, attn, conv, wo_bf, g_ffn, wup_bf, ffn_conv_w, prev, wd_bf, p, wple_bf, wpg_bf, g_final)


def _rope_tables(pos):
    half = ROT_DIM // 2
    inv = ROPE_THETA ** (-jnp.arange(0, ROT_DIM, 2, dtype=F32) / ROT_DIM)
    ang = pos.astype(F32)[:, None] * inv[None, :]
    cos, sin = jnp.cos(ang), jnp.sin(ang)
    ones = jnp.ones((pos.shape[0], HEAD_DIM - ROT_DIM), F32)
    zeros = jnp.zeros((pos.shape[0], HEAD_DIM - ROT_DIM), F32)
    zh = jnp.zeros_like(sin)
    tile = lambda parts: jnp.tile(jnp.concatenate(parts, axis=1), (1, LANES // HEAD_DIM))
    return tile([cos, cos, ones]), tile([-sin, zh, zeros]), tile([zh, sin, zeros])


def _pad_front(x, pad):
    return jnp.pad(x, ((pad - x.shape[0], 0), (0, 0)))


def _pick_tile(rows, want):
    return want if rows % want == 0 else rows


def kernel(x_prompt, x_sample, cache_k, cache_v, state_conv, state_ffn_conv, page_table,
           p_prompt, p_sample, g_attn, w_in, lambda_qk, g_subln, conv_w, w_o, g_ffn, w_up,
           ffn_conv_w, w_down, w_ple, w_pg, g_final):
    depth = w_in.shape[0]
    b, s, d = x_prompt.shape
    assert b == 1
    db, t, _ = x_sample.shape
    nh = cache_k.shape[3]
    aw = nh * V_DIM
    cwid = conv_w.shape[2]
    dff = w_down.shape[1]
    page = cache_k.shape[2]
    past_len = page_table.shape[1] * page
    n_phys = cache_k.shape[1]
    ckt = cache_k.transpose(0, 1, 3, 4, 5, 2).reshape(depth, n_phys, aw, page)

    tabs_p = _rope_tables(jnp.arange(s))
    tabs_s = _rope_tables(past_len + jnp.repeat(jnp.arange(t), db))
    pad_p = SUBLANES
    pad_s = _round_up(2 * db, SUBLANES)

    hp = x_prompt.reshape(s, d)
    hs = x_sample.transpose(1, 0, 2).reshape(t * db, d)
    to_tm = lambda x: x.transpose(1, 0, 2).reshape(x.shape[0] * x.shape[1], x.shape[2])
    from_tm = lambda x, rows: x.reshape(rows, db, x.shape[1]).transpose(1, 0, 2)

    tm_in = _pick_tile(s, 512)
    tm_post = _pick_tile(s, 256)
    tq = _pick_tile(s, 512)
    chunk = dff // 2 if (dff // 2) % LANES == 0 else dff
    npg = 8 if page_table.shape[1] % 8 == 0 else 1

    kp, vp, cp, fp, ksm, vsm, csm, fsm = [], [], [], [], [], [], [], []
    for i in range(depth):
        lam_init = 0.8 - 0.6 * math.exp(-0.3 * i)
        final = i == depth - 1
        win_bf, wo_bf, wup_bf = w_in[i].astype(BF16), w_o[i].astype(BF16), w_up[i].astype(BF16)
        wd_bf, wple_bf, wpg_bf = w_down[i].astype(BF16), w_ple[i].astype(BF16), w_pg[i].astype(BF16)
        ga, gf, gs, gfin = g_attn[i][None], g_ffn[i][None], g_subln[i][None], g_final[None]

        q, k, kb, v, vb, co, cst = _in_proj(
            hp, ga, win_bf, tabs_p, conv_w[i], jnp.zeros((pad_p, cwid), F32), tm=tm_in, stride=1)
        attn = _prompt_attention(q, kb, vb, lambda_qk[i], gs, tq=tq, lam_init=lam_init)
        hp, fst = _post(hp, attn, co, wo_bf, gf, wup_bf, ffn_conv_w[i], jnp.zeros((pad_p, dff), F32),
                        wd_bf, p_prompt[i].reshape(s, -1), wple_bf, wpg_bf, gfin,
                        tm=tm_post, stride=1, chunk=chunk, final=final)
        kp.append(k)
        vp.append(v)
        cp.append(cst[pad_p - 2:])
        fp.append(fst[pad_p - 2:])

        q, k, kb, v, vb, co, cst = _in_proj(
            hs, ga, win_bf, tabs_s, conv_w[i], _pad_front(to_tm(state_conv[i]), pad_s),
            tm=t * db, stride=db)
        q_rep = jnp.tile(from_tm(q, t).astype(F32), (1, 2 * nh, 1))
        k_new, v_new = from_tm(k, t), from_tm(v, t)
        attn = _sample_attention(page_table, q_rep, k_new, v_new, lambda_qk[i], gs, ckt, cache_v,
                                 layer=i, npg=npg, lam_init=lam_init)
        hs, fst = _post(hs, to_tm(attn).astype(BF16), co, wo_bf, gf, wup_bf, ffn_conv_w[i],
                        _pad_front(to_tm(state_ffn_conv[i]), pad_s), wd_bf, to_tm(p_sample[i]),
                        wple_bf, wpg_bf, gfin, tm=t * db, stride=db, chunk=chunk, final=final)
        ksm.append(k_new)
        vsm.append(v_new)
        csm.append(from_tm(cst[pad_s - 2 * db:], 2))
        fsm.append(from_tm(fst[pad_s - 2 * db:], 2))

    y_prompt = hp.reshape(b, s, d)
    y_sample = from_tm(hs, t)
    return (y_prompt, y_sample,
            jnp.stack(kp).reshape(depth, b, s, nh, 2, HEAD_DIM),
            jnp.stack(vp).reshape(depth, b, s, nh, V_DIM),
            jnp.stack(cp).reshape(depth, b, 2, cwid),
            jnp.stack(fp).reshape(depth, b, 2, dff),
            jnp.stack(ksm).reshape(depth, db, t, nh, 2, HEAD_DIM),
            jnp.stack(vsm).reshape(depth, db, t, nh, V_DIM),
            jnp.stack(csm), jnp.stack(fsm))
```

```python
import functools
import math

import jax
import jax.numpy as jnp
from jax import lax
from jax.experimental import pallas as pl
from jax.experimental.pallas import tpu as pltpu

HEAD_DIM = 64
HEAD_SHIFT = 6
V_DIM = 2 * HEAD_DIM
ROT_DIM = HEAD_DIM // 4
ROPE_THETA = 500000.0
CONV_K = 3
RMS_EPS = 1e-6
QK_SCALE = HEAD_DIM ** -0.5 * math.log2(math.e)
FLASH_ROWS = 32
FLASH_COLS = 256
LANES = 128
SUBLANES = 8
VMEM_LIMIT = 56 * 1024 * 1024

F32 = jnp.float32
BF16 = jnp.bfloat16
NT_DIMS = (((1,), (1,)), ((), ()))


def _round_up(x, m):
    return (x + m - 1) // m * m


def _rms(x, g):
    return x * lax.rsqrt(jnp.mean(x * x, axis=-1, keepdims=True) + RMS_EPS) * g


def _resident(shape):
    return pl.BlockSpec(shape, lambda *_: (0,) * len(shape), pipeline_mode=pl.Buffered(1))


def _diff_lambda(lamqk, lam_init):
    a = jnp.sum(lamqk[0:1] * lamqk[1:2], axis=1, keepdims=True)
    b = jnp.sum(lamqk[2:3] * lamqk[3:4], axis=1, keepdims=True)
    return jnp.exp(a) - jnp.exp(b) + lam_init


def _causal_conv3(buf_ref, cols, w, x, pad, stride, tm):
    x2 = buf_ref[pad - 2 * stride:pad - 2 * stride + tm, cols]
    x1 = buf_ref[pad - stride:pad - stride + tm, cols]
    return w[0:1] * x2 + w[1:2] * x1 + w[2:3] * x


def _in_proj_kernel(h_ref, g_ref, w_ref, cos_ref, sa_ref, sb_ref, cw_ref, prev_ref, *rest,
                    tm, pad, stride, aw, cwid, prompt):
    if prompt:
        qt_ref, k_ref, kb_ref, v_ref, vt_ref, co_ref, st_ref, cbuf = rest
    else:
        q_ref, k_ref, v_ref, co_ref, st_ref, cbuf = rest
    i = pl.program_id(0)
    a = _rms(h_ref[...], g_ref[...]).astype(BF16)

    def proj(lo, width):
        return jnp.dot(a, w_ref[:, lo:lo + width], preferred_element_type=F32)

    cos, sa, sb = cos_ref[...], sa_ref[...], sb_ref[...]
    half = ROT_DIM // 2

    def rope(x):
        outs = []
        for hh in range(aw // LANES):
            xs = x[:, hh * LANES:(hh + 1) * LANES]
            up = pltpu.roll(xs, LANES - half, axis=1)
            dn = pltpu.roll(xs, half, axis=1)
            outs.append(xs * cos + up * sa + dn * sb)
        return jnp.concatenate(outs, axis=1)

    q = rope(proj(0, aw)) * QK_SCALE
    k = rope(proj(aw, aw))
    k_ref[...] = k
    v = proj(2 * aw, aw)
    v_ref[...] = v
    if prompt:
        qt_ref[...] = q.T.astype(BF16)
        kb_ref[...] = k.astype(BF16)
        vt_ref[:, 0] = v.T.reshape(aw // V_DIM, V_DIM, tm).astype(BF16)
    else:
        q_ref[...] = q.astype(BF16)

    gb = proj(3 * aw, cwid)
    cu = proj(3 * aw + cwid, cwid) * proj(3 * aw + 2 * cwid, cwid)

    @pl.when(i == 0)
    def _():
        cbuf[0:pad, :] = prev_ref[...]

    cbuf[pad:pad + tm, :] = cu
    y = _causal_conv3(cbuf, slice(None), cw_ref[...], cu, pad, stride, tm)
    co_ref[...] = (gb * y).astype(BF16)
    tail = cbuf[tm:tm + pad, :]
    st_ref[...] = tail
    cbuf[0:pad, :] = tail


def _in_proj(h, g, w_bf, tabs, conv_w, prev, *, tm, stride, prompt):
    r, d = h.shape
    aw = cwid = conv_w.shape[1]
    nh = aw // V_DIM
    pad = prev.shape[0]
    cos, sa, sb = tabs
    row = lambda width: pl.BlockSpec((tm, width), lambda i: (i, 0))
    kern = functools.partial(_in_proj_kernel, tm=tm, pad=pad, stride=stride, aw=aw, cwid=cwid,
                             prompt=prompt)
    f32_rows = (row(aw), jax.ShapeDtypeStruct((r, aw), F32))
    bf_rows = (row(aw), jax.ShapeDtypeStruct((r, aw), BF16))
    conv_out = (row(cwid), jax.ShapeDtypeStruct((r, cwid), BF16))
    state = (pl.BlockSpec((pad, cwid), lambda i: (0, 0)),
             jax.ShapeDtypeStruct((pad, cwid), F32))
    if prompt:
        q_t = (pl.BlockSpec((aw, tm), lambda i: (0, i)), jax.ShapeDtypeStruct((aw, r), BF16))
        v_t = (pl.BlockSpec((nh, 1, V_DIM, tm), lambda i: (0, i, 0, 0)),
               jax.ShapeDtypeStruct((nh, r // tm, V_DIM, tm), BF16))
        outs = [q_t, f32_rows, bf_rows, f32_rows, v_t, conv_out, state]
    else:
        outs = [bf_rows, f32_rows, f32_rows, conv_out, state]
    return pl.pallas_call(
        kern,
        grid=(r // tm,),
        in_specs=[row(d), _resident((1, d)), _resident(w_bf.shape),
                  row(LANES), row(LANES), row(LANES),
                  _resident(conv_w.shape), _resident(prev.shape)],
        out_specs=[o[0] for o in outs],
        out_shape=[o[1] for o in outs],
        scratch_shapes=[pltpu.VMEM((pad + tm, cwid), F32)],
        compiler_params=pltpu.CompilerParams(dimension_semantics=("arbitrary",),
                                             vmem_limit_bytes=VMEM_LIMIT),
        name="in_proj",
    )(h, g, w_bf, cos, sa, sb, conv_w, prev)


def _flash_kernel(qt_ref, k_ref, vt_ref, lam_ref, g_ref, o_ref,
                  qz_ref, sa_ref, sb_ref, pa_ref, pb_ref, aa_ref, ab_ref, m_ref, l_ref, acc_ref,
                  *, tq, lam_init):
    i = pl.program_id(1)
    qt = qt_ref[...]
    row = lax.broadcasted_iota(jnp.int32, qt.shape, 0)
    zero = jnp.zeros_like(qt)
    qz_ref[:, 0:tq] = jnp.where(row < HEAD_DIM, qt, zero)
    qz_ref[:, tq:2 * tq] = jnp.where(row >= HEAD_DIM, qt, zero)
    m_ref[...] = jnp.full(m_ref.shape, -jnp.inf, F32)
    l_ref[...] = jnp.zeros(l_ref.shape, F32)
    acc_ref[...] = jnp.zeros(acc_ref.shape, F32)

    def softmax_cols(s_ref, p_ref, a_ref, c0, masked):
        cols = slice(c0, c0 + LANES)
        shape = (FLASH_ROWS, LANES)

        def scores(r0):
            s = s_ref[r0:r0 + FLASH_ROWS, cols]
            if masked:
                ki = r0 + lax.broadcasted_iota(jnp.int32, shape, 0)
                qi = (c0 % tq) + lax.broadcasted_iota(jnp.int32, shape, 1)
                s = jnp.where(ki <= qi, s, -jnp.inf)
            return s

        mx = scores(0)
        for r0 in range(FLASH_ROWS, tq, FLASH_ROWS):
            mx = jnp.maximum(mx, scores(r0))
        m_old = m_ref[:, cols]
        m_new = jnp.maximum(m_old, jnp.max(mx, axis=0, keepdims=True))
        lsum = jnp.zeros(shape, F32)
        for r0 in range(0, tq, FLASH_ROWS):
            p = jnp.exp2(scores(r0) - m_new)
            lsum = lsum + p
            p_ref[r0:r0 + FLASH_ROWS, cols] = p.astype(BF16)
        alpha = jnp.exp2(m_old - m_new)
        l_ref[:, cols] = alpha * l_ref[:, cols] + jnp.sum(lsum, axis=0, keepdims=True)
        m_ref[:, cols] = m_new
        a_ref[:, cols] = alpha

    def scores_into(s_ref, j):
        start = pl.multiple_of(j * tq, tq)
        s_ref[...] = jnp.dot(k_ref[pl.ds(start, tq), :], qz_ref[...],
                             preferred_element_type=F32)

    def softmax(bufs, masked):
        for c0 in range(0, 2 * tq, LANES):
            softmax_cols(*bufs, c0, masked)

    def accumulate(bufs, j):
        _, p_ref, a_ref = bufs
        acc_ref[...] = a_ref[...] * acc_ref[...] + jnp.dot(
            vt_ref[jnp.maximum(j, 0)], p_ref[...], preferred_element_type=F32)

    buf_a = (sa_ref, pa_ref, aa_ref)
    buf_b = (sb_ref, pb_ref, ab_ref)
    pb_ref[...] = jnp.zeros(pb_ref.shape, BF16)
    ab_ref[...] = jnp.zeros(ab_ref.shape, F32)
    scores_into(sa_ref, 0)

    def pair(jj, carry):
        j = 2 * jj
        scores_into(sb_ref, j + 1)
        accumulate(buf_b, j - 1)
        softmax(buf_a, False)
        scores_into(sa_ref, j + 2)
        accumulate(buf_a, j)
        softmax(buf_b, False)
        return carry

    lax.fori_loop(0, i // 2, pair, 0)

    @pl.when(i % 2 == 1)
    def _():
        scores_into(sb_ref, i)
        accumulate(buf_b, i - 2)
        softmax(buf_a, False)
        accumulate(buf_a, i - 1)
        softmax(buf_b, True)
        accumulate(buf_b, i)

    @pl.when(i % 2 == 0)
    def _():
        accumulate(buf_b, i - 1)
        softmax(buf_a, True)
        accumulate(buf_a, i)

    o = acc_ref[...] / l_ref[...]
    lam = _diff_lambda(lam_ref[...], lam_init)
    o = o[:, 0:tq] - lam * o[:, tq:2 * tq]
    o = o * lax.rsqrt(jnp.mean(o * o, axis=0, keepdims=True) + RMS_EPS) * g_ref[...]
    o_ref[...] = (o * (1.0 - lam_init)).T.astype(o_ref.dtype)


def _prompt_attention(q_t, k, v_t, lamqk, g_col, *, lam_init):
    aw, t = q_t.shape
    nh, nkv, _, tq = v_t.shape
    return pl.pallas_call(
        functools.partial(_flash_kernel, tq=tq, lam_init=lam_init),
        grid=(nh, t // tq),
        in_specs=[pl.BlockSpec((V_DIM, tq), lambda h, i: (h, i)),
                  pl.BlockSpec((t, V_DIM), lambda h, i: (0, h)),
                  pl.BlockSpec((None, nkv, V_DIM, tq), lambda h, i: (h, 0, 0, 0)),
                  _resident(lamqk.shape), _resident(g_col.shape)],
        out_specs=pl.BlockSpec((tq, V_DIM), lambda h, i: (i, h)),
        out_shape=jax.ShapeDtypeStruct((t, aw), BF16),
        scratch_shapes=[pltpu.VMEM((V_DIM, 2 * tq), BF16),
                        pltpu.VMEM((tq, 2 * tq), F32),
                        pltpu.VMEM((tq, 2 * tq), F32),
                        pltpu.VMEM((tq, 2 * tq), BF16),
                        pltpu.VMEM((tq, 2 * tq), BF16),
                        pltpu.VMEM((1, 2 * tq), F32),
                        pltpu.VMEM((1, 2 * tq), F32),
                        pltpu.VMEM((1, 2 * tq), F32),
                        pltpu.VMEM((1, 2 * tq), F32),
                        pltpu.VMEM((V_DIM, 2 * tq), F32)],
        compiler_params=pltpu.CompilerParams(dimension_semantics=("arbitrary", "arbitrary"),
                                             vmem_limit_bytes=VMEM_LIMIT),
        name="prompt_attention",
    )(q_t, k, v_t, lamqk, g_col)


def _decode_kernel(pt_ref, q_ref, kn_ref, vn_ref, lam_ref, g_ref, *rest, npg, nt, nh, lam_init):
    kt_refs = rest[:npg]
    v_refs = rest[npg:2 * npg]
    o_ref, qbd_ref, m_ref, l_ref, acc_ref = rest[2 * npg:]
    c = pl.program_id(1)
    nrow, aw = qbd_ref.shape
    hrows = nrow // nh

    @pl.when(c == 0)
    def _():
        qrep = q_ref[0]
        r = lax.broadcasted_iota(jnp.int32, (nrow, aw), 0)
        lo = (lax.broadcasted_iota(jnp.int32, (nrow, aw), 1) >> HEAD_SHIFT) * nt
        qbd_ref[...] = jnp.where((r >= lo) & (r < lo + nt), qrep, 0.0).astype(BF16)
        m_ref[...] = jnp.full(m_ref.shape, -jnp.inf, F32)
        l_ref[...] = jnp.zeros(l_ref.shape, F32)
        acc_ref[...] = jnp.zeros(acc_ref.shape, F32)

    qbd = qbd_ref[...]
    s = jnp.concatenate(
        [jnp.dot(qbd, kt[...].astype(BF16), preferred_element_type=F32) for kt in kt_refs],
        axis=1)
    m_prev = m_ref[...]
    m_new = jnp.maximum(m_prev, jnp.max(s, axis=1, keepdims=True))
    alpha = jnp.exp2(m_prev - m_new)
    p = jnp.exp2(s - m_new)
    l_ref[...] = alpha * l_ref[...] + jnp.sum(p, axis=1, keepdims=True)
    page = s.shape[1] // npg
    pv = []
    for hh in range(nh):
        ph = p[hh * hrows:(hh + 1) * hrows].astype(BF16)
        pvh = jnp.zeros((hrows, V_DIM), F32)
        for j in range(npg):
            vh = v_refs[j][pl.ds(hh, page, stride=nh), :]
            pvh += jnp.dot(ph[:, j * page:(j + 1) * page], vh.astype(BF16),
                           preferred_element_type=F32)
        pv.append(pvh)
    acc_ref[...] = alpha * acc_ref[...] + jnp.concatenate(pv, axis=0)
    m_ref[...] = m_new

    @pl.when(c == pl.num_programs(1) - 1)
    def _():
        qf = qbd.astype(F32)
        kn, vn = kn_ref[0], vn_ref[0]
        tq = lax.broadcasted_iota(jnp.int32, (nrow, 1), 0) & (nt - 1)
        s_new = []
        for tk in range(nt):
            st = jnp.sum(qf * kn[tk:tk + 1, :], axis=1, keepdims=True)
            s_new.append(jnp.where(tk <= tq, st, -jnp.inf))
        m_fin = m_new
        for st in s_new:
            m_fin = jnp.maximum(m_fin, st)
        a_fin = jnp.exp2(m_new - m_fin)
        l = a_fin * l_ref[...]
        acc = a_fin * acc_ref[...]
        for tk in range(nt):
            pn = jnp.exp2(s_new[tk] - m_fin)
            l = l + pn
            v_rows = jnp.concatenate(
                [jnp.broadcast_to(vn[tk:tk + 1, hh * V_DIM:(hh + 1) * V_DIM], (hrows, V_DIM))
                 for hh in range(nh)], axis=0)
            acc = acc + pn * v_rows
        acc_ref[...] = acc / l
        lam = _diff_lambda(lam_ref[...], lam_init)
        g = g_ref[...]
        for hh in range(nh):
            r0 = hrows * hh
            o = acc_ref[r0:r0 + nt, :] - lam * acc_ref[r0 + nt:r0 + 2 * nt, :]
            o_ref[0, :, hh * V_DIM:(hh + 1) * V_DIM] = _rms(o, g) * (1.0 - lam_init)


def _sample_attention(page_table, q_rep, k_new, v_new, lamqk, g_sub, cache_kt, cache_v,
                      *, layer, npg, lam_init):
    nb, nt, aw = k_new.shape
    n_pages = page_table.shape[1]
    page = cache_kt.shape[3]
    nh = aw // V_DIM
    nrow = q_rep.shape[1]
    pt_flat = page_table.reshape(-1)

    def phys(b, c, pt, j):
        return pt[b * n_pages + c * npg + j]

    def kt_spec(j):
        return pl.BlockSpec((None, None, aw, page), lambda b, c, pt: (layer, phys(b, c, pt, j), 0, 0))

    def v_spec(j):
        return pl.BlockSpec((None, None, page * nh, V_DIM),
                            lambda b, c, pt: (layer, phys(b, c, pt, j), 0, 0))

    per_seq = lambda rows: pl.BlockSpec((1, rows, aw), lambda b, c, pt: (b, 0, 0))
    const = lambda shape: pl.BlockSpec(shape, lambda b, c, pt: (0,) * len(shape))
    grid_spec = pltpu.PrefetchScalarGridSpec(
        num_scalar_prefetch=1,
        grid=(nb, n_pages // npg),
        in_specs=[per_seq(nrow), per_seq(nt), per_seq(nt), const(lamqk.shape), const(g_sub.shape)]
                 + [kt_spec(j) for j in range(npg)] + [v_spec(j) for j in range(npg)],
        out_specs=per_seq(nt),
        scratch_shapes=[pltpu.VMEM((nrow, aw), BF16), pltpu.VMEM((nrow, 1), F32),
                        pltpu.VMEM((nrow, 1), F32), pltpu.VMEM((nrow, V_DIM), F32)],
    )
    return pl.pallas_call(
        functools.partial(_decode_kernel, npg=npg, nt=nt, nh=nh, lam_init=lam_init),
        grid_spec=grid_spec,
        out_shape=jax.ShapeDtypeStruct((nb, nt, aw), F32),
        compiler_params=pltpu.CompilerParams(dimension_semantics=("arbitrary", "arbitrary"),
                                             vmem_limit_bytes=VMEM_LIMIT),
        name="sample_attention",
    )(pt_flat, q_rep, k_new, v_new, lamqk, g_sub, *([cache_kt] * npg), *([cache_v] * npg))


def _post_kernel(h_ref, at_ref, co_ref, wo_ref, gf_ref, wup_ref, fcw_ref, prev_ref, wd_ref,
                 p_ref, wple_ref, wpg_ref, gfin_ref, out_ref, st_ref, gbuf,
                 *, tm, pad, stride, dff, chunk, final):
    i = pl.program_id(0)
    cat = jnp.concatenate([at_ref[...], co_ref[...]], axis=1)
    h1 = h_ref[...] + jnp.dot(cat, wo_ref[...], preferred_element_type=F32)
    f = _rms(h1, gf_ref[...]).astype(BF16)

    @pl.when(i == 0)
    def _():
        gbuf[0:pad, :] = prev_ref[...]

    acc = h1
    for lo in range(0, dff, chunk):
        cols = slice(lo, lo + chunk)
        gate = jnp.dot(f, wup_ref[:, lo:lo + chunk], preferred_element_type=F32)
        up = jnp.dot(f, wup_ref[:, dff + lo:dff + lo + chunk], preferred_element_type=F32)
        gbuf[pad:pad + tm, cols] = gate
        gate_c = _causal_conv3(gbuf, cols, fcw_ref[:, cols], gate, pad, stride, tm)
        act = (gate_c * jax.nn.sigmoid(gate_c) * up).astype(BF16)
        acc = acc + jnp.dot(act, wd_ref[lo:lo + chunk, :], preferred_element_type=F32)

    tail = gbuf[tm:tm + pad, :]
    st_ref[...] = tail
    gbuf[0:pad, :] = tail

    h2 = acc
    ple = jnp.dot(p_ref[...].astype(BF16), wple_ref[...], preferred_element_type=F32)
    pg = jnp.dot(h2.astype(BF16), wpg_ref[...], preferred_element_type=F32)
    h3 = h2 + ple * jax.nn.sigmoid(pg)
    out_ref[...] = _rms(h3, gfin_ref[...]) if final else h3


def _post(h, attn, conv, wo_bf, g_ffn, wup_bf, ffn_conv_w, prev, wd_bf, p, wple_bf, wpg_bf, g_final,
          *, tm, stride, chunk, final):
    r, d = h.shape
    dff = wd_bf.shape[0]
    pad = prev.shape[0]
    row = lambda width: pl.BlockSpec((tm, width), lambda i: (i, 0))
    kern = functools.partial(_post_kernel, tm=tm, pad=pad, stride=stride, dff=dff, chunk=chunk,
                             final=final)
    return pl.pallas_call(
        kern,
        grid=(r // tm,),
        in_specs=[row(d), row(attn.shape[1]), row(conv.shape[1]), _resident(wo_bf.shape),
                  _resident((1, d)), _resident(wup_bf.shape), _resident(ffn_conv_w.shape),
                  _resident(prev.shape), _resident(wd_bf.shape), row(p.shape[1]),
                  _resident(wple_bf.shape), _resident(wpg_bf.shape), _resident((1, d))],
        out_specs=[row(d), pl.BlockSpec((pad, dff), lambda i: (0, 0))],
        out_shape=[jax.ShapeDtypeStruct((r, d), F32),
                   jax.ShapeDtypeStruct((pad, dff), F32)],
        scratch_shapes=[pltpu.VMEM((pad + tm, dff), F32)],
        compiler_params=pltpu.CompilerParams(dimension_semantics=("arbitrary",),
                                             vmem_limit_bytes=VMEM_LIMIT),
        name="post",
    )(h, attn, conv, wo_bf, g_ffn, wup_bf, ffn_conv_w, prev, wd_bf, p, wple_bf, wpg_bf, g_final)


def _rope_tables(pos):
    half = ROT_DIM // 2
    inv = ROPE_THETA ** (-jnp.arange(0, ROT_DIM, 2, dtype=F32) / ROT_DIM)
    ang = pos.astype(F32)[:, None] * inv[None, :]
    cos, sin = jnp.cos(ang), jnp.sin(ang)
    ones = jnp.ones((pos.shape[0], HEAD_DIM - ROT_DIM), F32)
    zeros = jnp.zeros((pos.shape[0], HEAD_DIM - ROT_DIM), F32)
    zh = jnp.zeros_like(sin)
    tile = lambda parts: jnp.tile(jnp.concatenate(parts, axis=1), (1, LANES // HEAD_DIM))
    return tile([cos, cos, ones]), tile([-sin, zh, zeros]), tile([zh, sin, zeros])


def _pad_front(x, pad):
    return jnp.pad(x, ((pad - x.shape[0], 0), (0, 0)))


def _pick_tile(rows, want):
    return want if rows % want == 0 else rows


def kernel(x_prompt, x_sample, cache_k, cache_v, state_conv, state_ffn_conv, page_table,
           p_prompt, p_sample, g_attn, w_in, lambda_qk, g_subln, conv_w, w_o, g_ffn, w_up,
           ffn_conv_w, w_down, w_ple, w_pg, g_final):
    depth = w_in.shape[0]
    b, s, d = x_prompt.shape
    assert b == 1
    db, t, _ = x_sample.shape
    nh = cache_k.shape[3]
    aw = nh * V_DIM
    cwid = conv_w.shape[2]
    dff = w_down.shape[1]
    page = cache_k.shape[2]
    past_len = page_table.shape[1] * page
    n_phys = cache_k.shape[1]
    ckt = cache_k.transpose(0, 1, 3, 4, 5, 2).reshape(depth, n_phys, aw, page)
    cv2 = cache_v.reshape(depth, n_phys, page * nh, V_DIM)

    tabs_p = _rope_tables(jnp.arange(s))
    tabs_s = _rope_tables(past_len + jnp.repeat(jnp.arange(t), db))
    pad_p = SUBLANES
    pad_s = _round_up(2 * db, SUBLANES)

    hp = x_prompt.reshape(s, d)
    hs = x_sample.transpose(1, 0, 2).reshape(t * db, d)
    to_tm = lambda x: x.transpose(1, 0, 2).reshape(x.shape[0] * x.shape[1], x.shape[2])
    from_tm = lambda x, rows: x.reshape(rows, db, x.shape[1]).transpose(1, 0, 2)

    tm_post = _pick_tile(s, 256)
    tq = _pick_tile(s, 512)
    chunk = dff
    npg = next(n for n in (16, 8, 4, 2, 1) if page_table.shape[1] % n == 0)

    kp, vp, cp, fp, ksm, vsm, csm, fsm = [], [], [], [], [], [], [], []
    for i in range(depth):
        lam_init = 0.8 - 0.6 * math.exp(-0.3 * i)
        final = i == depth - 1
        win_bf, wo_bf, wup_bf = w_in[i].astype(BF16), w_o[i].astype(BF16), w_up[i].astype(BF16)
        wd_bf, wple_bf, wpg_bf = w_down[i].astype(BF16), w_ple[i].astype(BF16), w_pg[i].astype(BF16)
        ga, gf, gs, gfin = g_attn[i][None], g_ffn[i][None], g_subln[i][None], g_final[None]

        q_t, k, kb, v, v_t, co, cst = _in_proj(
            hp, ga, win_bf, tabs_p, conv_w[i], jnp.zeros((pad_p, cwid), F32), tm=tq, stride=1,
            prompt=True)
        attn = _prompt_attention(q_t, kb, v_t, lambda_qk[i], g_subln[i][:, None], lam_init=lam_init)
        hp, fst = _post(hp, attn, co, wo_bf, gf, wup_bf, ffn_conv_w[i], jnp.zeros((pad_p, dff), F32),
                        wd_bf, p_prompt[i].reshape(s, -1), wple_bf, wpg_bf, gfin,
                        tm=tm_post, stride=1, chunk=chunk, final=final)
        kp.append(k)
        vp.append(v)
        cp.append(cst[pad_p - 2:])
        fp.append(fst[pad_p - 2:])

        q, k, v, co, cst = _in_proj(
            hs, ga, win_bf, tabs_s, conv_w[i], _pad_front(to_tm(state_conv[i]), pad_s),
            tm=t * db, stride=db, prompt=False)
        q_rep = jnp.tile(from_tm(q, t).astype(F32), (1, 2 * nh, 1))
        k_new, v_new = from_tm(k, t), from_tm(v, t)
        attn = _sample_attention(page_table, q_rep, k_new, v_new, lambda_qk[i], gs, ckt, cv2,
                                 layer=i, npg=npg, lam_init=lam_init)
        hs, fst = _post(hs, to_tm(attn).astype(BF16), co, wo_bf, gf, wup_bf, ffn_conv_w[i],
                        _pad_front(to_tm(state_ffn_conv[i]), pad_s), wd_bf, to_tm(p_sample[i]),
                        wple_bf, wpg_bf, gfin, tm=t * db, stride=db, chunk=chunk, final=final)
        ksm.append(k_new)
        vsm.append(v_new)
        csm.append(from_tm(cst[pad_s - 2 * db:], 2))
        fsm.append(from_tm(fst[pad_s - 2 * db:], 2))

    y_prompt = hp.reshape(b, s, d)
    y_sample = from_tm(hs, t)
    return (y_prompt, y_sample,
            jnp.stack(kp).reshape(depth, b, s, nh, 2, HEAD_DIM),
            jnp.stack(vp).reshape(depth, b, s, nh, V_DIM),
            jnp.stack(cp).reshape(depth, b, 2, cwid),
            jnp.stack(fp).reshape(depth, b, 2, dff),
            jnp.stack(ksm).reshape(depth, db, t, nh, 2, HEAD_DIM),
            jnp.stack(vsm).reshape(depth, db, t, nh, V_DIM),
            jnp.stack(csm), jnp.stack(fsm))
```

```python
import functools
import math

import jax
import jax.numpy as jnp
from jax import lax
from jax.experimental import pallas as pl
from jax.experimental.pallas import tpu as pltpu

HEAD_DIM = 64
HEAD_SHIFT = 6
V_DIM = 2 * HEAD_DIM
ROT_DIM = HEAD_DIM // 4
ROPE_THETA = 500000.0
CONV_K = 3
RMS_EPS = 1e-6
QK_SCALE = HEAD_DIM ** -0.5 * math.log2(math.e)
FLASH_ROWS = 16
FLASH_COLS = 256
LANES = 128
SUBLANES = 8
VMEM_LIMIT = 56 * 1024 * 1024

F32 = jnp.float32
BF16 = jnp.bfloat16
NT_DIMS = (((1,), (1,)), ((), ()))


def _round_up(x, m):
    return (x + m - 1) // m * m


def _rms(x, g):
    return x * lax.rsqrt(jnp.mean(x * x, axis=-1, keepdims=True) + RMS_EPS) * g


def _resident(shape, layer=None):
    if layer is None:
        return pl.BlockSpec(shape, lambda *_: (0,) * len(shape), pipeline_mode=pl.Buffered(1))
    return pl.BlockSpec((None,) + tuple(shape), lambda *_: (layer,) + (0,) * len(shape),
                        pipeline_mode=pl.Buffered(1))


def _diff_lambda(lamqk, lam_init):
    a = jnp.sum(lamqk[0:1] * lamqk[1:2], axis=1, keepdims=True)
    b = jnp.sum(lamqk[2:3] * lamqk[3:4], axis=1, keepdims=True)
    return jnp.exp(a) - jnp.exp(b) + lam_init


def _causal_conv3(buf_ref, cols, w, x, pad, stride, tm):
    x2 = buf_ref[pad - 2 * stride:pad - 2 * stride + tm, cols]
    x1 = buf_ref[pad - stride:pad - stride + tm, cols]
    return w[0:1] * x2 + w[1:2] * x1 + w[2:3] * x


def _in_proj_kernel(h_ref, g_ref, w_ref, cos_ref, sa_ref, sb_ref, cw_ref, prev_ref, *rest,
                    tm, pad, stride, aw, cwid, prompt):
    if prompt:
        qt_ref, k_ref, kb_ref, v_ref, vt_ref, co_ref, st_ref, cbuf = rest
    else:
        q_ref, k_ref, v_ref, co_ref, st_ref, cbuf = rest
    i = pl.program_id(0)
    a = _rms(h_ref[...], g_ref[...]).astype(BF16)

    def proj(lo, width):
        return jnp.dot(a, w_ref[:, lo:lo + width], preferred_element_type=F32)

    cos, sa, sb = cos_ref[...], sa_ref[...], sb_ref[...]
    half = ROT_DIM // 2

    def rope(x):
        outs = []
        for hh in range(aw // LANES):
            xs = x[:, hh * LANES:(hh + 1) * LANES]
            up = pltpu.roll(xs, LANES - half, axis=1)
            dn = pltpu.roll(xs, half, axis=1)
            outs.append(xs * cos + up * sa + dn * sb)
        return jnp.concatenate(outs, axis=1)

    nh = aw // V_DIM
    q = rope(proj(0, aw)) * QK_SCALE
    k = rope(proj(aw, aw))
    v = proj(2 * aw, aw)
    if prompt:
        qt_ref[...] = q.T.astype(BF16)
        k_ref[...] = k.T
        kb_ref[...] = k.astype(BF16)
        for hh in range(nh):
            v_ref[pl.ds(hh, tm, stride=nh), :] = v[:, hh * V_DIM:(hh + 1) * V_DIM]
        vt_ref[:, 0] = v.T.reshape(nh, V_DIM, tm).astype(BF16)
    else:
        q_ref[...] = q.astype(BF16)
        k_ref[...] = k
        v_ref[...] = v

    gb = proj(3 * aw, cwid)
    cu = proj(3 * aw + cwid, cwid) * proj(3 * aw + 2 * cwid, cwid)

    @pl.when(i == 0)
    def _():
        cbuf[0:pad, :] = prev_ref[...]

    cbuf[pad:pad + tm, :] = cu
    y = _causal_conv3(cbuf, slice(None), cw_ref[...], cu, pad, stride, tm)
    co_ref[...] = (gb * y).astype(BF16)
    tail = cbuf[tm:tm + pad, :]
    st_ref[...] = tail
    cbuf[0:pad, :] = tail


def _in_proj(h, g, w_bf, tabs, conv_w, prev, *, layer, tm, stride, prompt):
    r, d = h.shape
    aw = cwid = conv_w.shape[2]
    nh = aw // V_DIM
    pad = prev.shape[0]
    cos, sa, sb = tabs
    row = lambda width: pl.BlockSpec((tm, width), lambda i: (i, 0))
    kern = functools.partial(_in_proj_kernel, tm=tm, pad=pad, stride=stride, aw=aw, cwid=cwid,
                             prompt=prompt)
    f32_rows = (row(aw), jax.ShapeDtypeStruct((r, aw), F32))
    bf_rows = (row(aw), jax.ShapeDtypeStruct((r, aw), BF16))
    conv_out = (row(cwid), jax.ShapeDtypeStruct((r, cwid), BF16))
    state = (pl.BlockSpec((pad, cwid), lambda i: (0, 0)),
             jax.ShapeDtypeStruct((pad, cwid), F32))
    if prompt:
        q_t = (pl.BlockSpec((aw, tm), lambda i: (0, i)), jax.ShapeDtypeStruct((aw, r), BF16))
        v_t = (pl.BlockSpec((nh, 1, V_DIM, tm), lambda i: (0, i, 0, 0)),
               jax.ShapeDtypeStruct((nh, r // tm, V_DIM, tm), BF16))
        k_t = (pl.BlockSpec((aw, tm), lambda i: (0, i)), jax.ShapeDtypeStruct((aw, r), F32))
        v_rows = (pl.BlockSpec((tm * nh, V_DIM), lambda i: (i, 0)),
                  jax.ShapeDtypeStruct((r * nh, V_DIM), F32))
        outs = [q_t, k_t, bf_rows, v_rows, v_t, conv_out, state]
    else:
        outs = [bf_rows, f32_rows, f32_rows, conv_out, state]
    return pl.pallas_call(
        kern,
        grid=(r // tm,),
        in_specs=[row(d), _resident((1, d), layer), _resident(w_bf.shape[1:], layer),
                  row(LANES), row(LANES), row(LANES),
                  _resident(conv_w.shape[1:], layer), _resident(prev.shape)],
        out_specs=[o[0] for o in outs],
        out_shape=[o[1] for o in outs],
        scratch_shapes=[pltpu.VMEM((pad + tm, cwid), F32)],
        compiler_params=pltpu.CompilerParams(dimension_semantics=("arbitrary",),
                                             vmem_limit_bytes=VMEM_LIMIT),
        name="in_proj",
    )(h, g, w_bf, cos, sa, sb, conv_w, prev)


def _flash_kernel(qt_ref, k_ref, vt_ref, lam_ref, g_ref, o_ref,
                  qz_ref, sa_ref, sb_ref, pa_ref, pb_ref, aa_ref, ab_ref, m_ref, l_ref, acc_ref,
                  *, tq, lam_init):
    i = pl.program_id(1)
    qt = qt_ref[...]
    row = lax.broadcasted_iota(jnp.int32, qt.shape, 0)
    zero = jnp.zeros_like(qt)
    qz_ref[:, 0:tq] = jnp.where(row < HEAD_DIM, qt, zero)
    qz_ref[:, tq:2 * tq] = jnp.where(row >= HEAD_DIM, qt, zero)
    m_ref[...] = jnp.full(m_ref.shape, -jnp.inf, F32)
    l_ref[...] = jnp.zeros(l_ref.shape, F32)
    acc_ref[...] = jnp.zeros(acc_ref.shape, F32)

    def softmax_cols(s_ref, p_ref, a_ref, c0, masked):
        cols = slice(c0, c0 + LANES)
        shape = (FLASH_ROWS, LANES)

        def scores(r0):
            s = s_ref[r0:r0 + FLASH_ROWS, cols]
            if masked:
                ki = r0 + lax.broadcasted_iota(jnp.int32, shape, 0)
                qi = (c0 % tq) + lax.broadcasted_iota(jnp.int32, shape, 1)
                s = jnp.where(ki <= qi, s, -jnp.inf)
            return s

        mx = scores(0)
        for r0 in range(FLASH_ROWS, tq, FLASH_ROWS):
            mx = jnp.maximum(mx, scores(r0))
        m_old = m_ref[:, cols]
        m_new = jnp.maximum(m_old, jnp.max(mx, axis=0, keepdims=True))
        lsum = jnp.zeros(shape, F32)
        for r0 in range(0, tq, FLASH_ROWS):
            p = jnp.exp2(scores(r0) - m_new)
            lsum = lsum + p
            p_ref[r0:r0 + FLASH_ROWS, cols] = p.astype(BF16)
        alpha = jnp.exp2(m_old - m_new)
        l_ref[:, cols] = alpha * l_ref[:, cols] + jnp.sum(lsum, axis=0, keepdims=True)
        m_ref[:, cols] = m_new
        a_ref[:, cols] = alpha

    def scores_into(s_ref, j):
        start = pl.multiple_of(j * tq, tq)
        s_ref[...] = jnp.dot(k_ref[pl.ds(start, tq), :], qz_ref[...],
                             preferred_element_type=F32)

    def softmax(bufs, masked):
        for c0 in range(0, 2 * tq, LANES):
            softmax_cols(*bufs, c0, masked)

    def accumulate(bufs, j):
        _, p_ref, a_ref = bufs
        acc_ref[...] = a_ref[...] * acc_ref[...] + jnp.dot(
            vt_ref[jnp.maximum(j, 0)], p_ref[...], preferred_element_type=F32)

    buf_a = (sa_ref, pa_ref, aa_ref)
    buf_b = (sb_ref, pb_ref, ab_ref)
    pb_ref[...] = jnp.zeros(pb_ref.shape, BF16)
    ab_ref[...] = jnp.zeros(ab_ref.shape, F32)
    scores_into(sa_ref, 0)

    def pair(jj, carry):
        j = 2 * jj
        scores_into(sb_ref, j + 1)
        accumulate(buf_b, j - 1)
        softmax(buf_a, False)
        scores_into(sa_ref, j + 2)
        accumulate(buf_a, j)
        softmax(buf_b, False)
        return carry

    lax.fori_loop(0, i // 2, pair, 0)

    @pl.when(i % 2 == 1)
    def _():
        scores_into(sb_ref, i)
        accumulate(buf_b, i - 2)
        softmax(buf_a, False)
        accumulate(buf_a, i - 1)
        softmax(buf_b, True)
        accumulate(buf_b, i)

    @pl.when(i % 2 == 0)
    def _():
        accumulate(buf_b, i - 1)
        softmax(buf_a, True)
        accumulate(buf_a, i)

    o = acc_ref[...] / l_ref[...]
    lam = _diff_lambda(lam_ref[...], lam_init)
    o = o[:, 0:tq] - lam * o[:, tq:2 * tq]
    o = o * lax.rsqrt(jnp.mean(o * o, axis=0, keepdims=True) + RMS_EPS) * g_ref[...]
    o_ref[...] = (o * (1.0 - lam_init)).T.astype(o_ref.dtype)


def _prompt_attention(q_t, k, v_t, lamqk, g_col, *, lam_init):
    aw, t = q_t.shape
    nh, nkv, _, tq = v_t.shape
    return pl.pallas_call(
        functools.partial(_flash_kernel, tq=tq, lam_init=lam_init),
        grid=(nh, t // tq),
        in_specs=[pl.BlockSpec((V_DIM, tq), lambda h, i: (h, i)),
                  pl.BlockSpec((t, V_DIM), lambda h, i: (0, h)),
                  pl.BlockSpec((None, nkv, V_DIM, tq), lambda h, i: (h, 0, 0, 0)),
                  _resident(lamqk.shape), _resident(g_col.shape)],
        out_specs=pl.BlockSpec((tq, V_DIM), lambda h, i: (i, h)),
        out_shape=jax.ShapeDtypeStruct((t, aw), BF16),
        scratch_shapes=[pltpu.VMEM((V_DIM, 2 * tq), BF16),
                        pltpu.VMEM((tq, 2 * tq), F32),
                        pltpu.VMEM((tq, 2 * tq), F32),
                        pltpu.VMEM((tq, 2 * tq), BF16),
                        pltpu.VMEM((tq, 2 * tq), BF16),
                        pltpu.VMEM((1, 2 * tq), F32),
                        pltpu.VMEM((1, 2 * tq), F32),
                        pltpu.VMEM((1, 2 * tq), F32),
                        pltpu.VMEM((1, 2 * tq), F32),
                        pltpu.VMEM((V_DIM, 2 * tq), F32)],
        compiler_params=pltpu.CompilerParams(dimension_semantics=("arbitrary", "arbitrary"),
                                             vmem_limit_bytes=VMEM_LIMIT),
        name="prompt_attention",
    )(q_t, k, v_t, lamqk, g_col)


def _decode_kernel(pt_ref, q_ref, kn_ref, vn_ref, lam_ref, g_ref, *rest, npg, nt, nh, lam_init):
    kt_refs = rest[:npg]
    v_refs = rest[npg:2 * npg]
    o_ref, qbd_ref, m_ref, l_ref, acc_ref = rest[2 * npg:]
    c = pl.program_id(1)
    nrow, aw = qbd_ref.shape
    hrows = nrow // nh

    @pl.when(c == 0)
    def _():
        qrep = q_ref[0]
        r = lax.broadcasted_iota(jnp.int32, (nrow, aw), 0)
        lo = (lax.broadcasted_iota(jnp.int32, (nrow, aw), 1) >> HEAD_SHIFT) * nt
        qbd_ref[...] = jnp.where((r >= lo) & (r < lo + nt), qrep, 0.0).astype(BF16)
        m_ref[...] = jnp.full(m_ref.shape, -jnp.inf, F32)
        l_ref[...] = jnp.zeros(l_ref.shape, F32)
        acc_ref[...] = jnp.zeros(acc_ref.shape, F32)

    qbd = qbd_ref[...]
    s = jnp.concatenate(
        [jnp.dot(qbd, kt[...].astype(BF16), preferred_element_type=F32) for kt in kt_refs],
        axis=1)
    m_prev = m_ref[...]
    m_new = jnp.maximum(m_prev, jnp.max(s, axis=1, keepdims=True))
    alpha = jnp.exp2(m_prev - m_new)
    p = jnp.exp2(s - m_new)
    l_ref[...] = alpha * l_ref[...] + jnp.sum(p, axis=1, keepdims=True)
    page = s.shape[1] // npg
    pv = []
    for hh in range(nh):
        ph = p[hh * hrows:(hh + 1) * hrows].astype(BF16)
        pvh = jnp.zeros((hrows, V_DIM), F32)
        for j in range(npg):
            vh = v_refs[j][pl.ds(hh, page, stride=nh), :]
            pvh += jnp.dot(ph[:, j * page:(j + 1) * page], vh.astype(BF16),
                           preferred_element_type=F32)
        pv.append(pvh)
    acc_ref[...] = alpha * acc_ref[...] + jnp.concatenate(pv, axis=0)
    m_ref[...] = m_new

    @pl.when(c == pl.num_programs(1) - 1)
    def _():
        qf = qbd.astype(F32)
        kn, vn = kn_ref[0], vn_ref[0]
        tq = lax.broadcasted_iota(jnp.int32, (nrow, 1), 0) & (nt - 1)
        s_new = []
        for tk in range(nt):
            st = jnp.sum(qf * kn[tk:tk + 1, :], axis=1, keepdims=True)
            s_new.append(jnp.where(tk <= tq, st, -jnp.inf))
        m_fin = m_new
        for st in s_new:
            m_fin = jnp.maximum(m_fin, st)
        a_fin = jnp.exp2(m_new - m_fin)
        l = a_fin * l_ref[...]
        acc = a_fin * acc_ref[...]
        for tk in range(nt):
            pn = jnp.exp2(s_new[tk] - m_fin)
            l = l + pn
            v_rows = jnp.concatenate(
                [jnp.broadcast_to(vn[tk:tk + 1, hh * V_DIM:(hh + 1) * V_DIM], (hrows, V_DIM))
                 for hh in range(nh)], axis=0)
            acc = acc + pn * v_rows
        acc_ref[...] = acc / l
        lam = _diff_lambda(lam_ref[...], lam_init)
        g = g_ref[...]
        for hh in range(nh):
            r0 = hrows * hh
            o = acc_ref[r0:r0 + nt, :] - lam * acc_ref[r0 + nt:r0 + 2 * nt, :]
            o_ref[0, :, hh * V_DIM:(hh + 1) * V_DIM] = _rms(o, g) * (1.0 - lam_init)


def _sample_attention(page_table, q_rep, k_new, v_new, lamqk, g_sub, cache_kt, cache_v,
                      *, layer, npg, lam_init):
    nb, nt, aw = k_new.shape
    n_pages = page_table.shape[1]
    page = cache_kt.shape[3]
    nh = aw // V_DIM
    nrow = q_rep.shape[1]
    pt_flat = page_table.reshape(-1)

    def phys(b, c, pt, j):
        return pt[b * n_pages + c * npg + j]

    def kt_spec(j):
        return pl.BlockSpec((None, None, aw, page), lambda b, c, pt: (layer, phys(b, c, pt, j), 0, 0))

    def v_spec(j):
        return pl.BlockSpec((None, None, page * nh, V_DIM),
                            lambda b, c, pt: (layer, phys(b, c, pt, j), 0, 0))

    per_seq = lambda rows: pl.BlockSpec((1, rows, aw), lambda b, c, pt: (b, 0, 0))
    const = lambda shape: pl.BlockSpec(shape, lambda b, c, pt: (0,) * len(shape))
    grid_spec = pltpu.PrefetchScalarGridSpec(
        num_scalar_prefetch=1,
        grid=(nb, n_pages // npg),
        in_specs=[per_seq(nrow), per_seq(nt), per_seq(nt), const(lamqk.shape), const(g_sub.shape)]
                 + [kt_spec(j) for j in range(npg)] + [v_spec(j) for j in range(npg)],
        out_specs=per_seq(nt),
        scratch_shapes=[pltpu.VMEM((nrow, aw), BF16), pltpu.VMEM((nrow, 1), F32),
                        pltpu.VMEM((nrow, 1), F32), pltpu.VMEM((nrow, V_DIM), F32)],
    )
    return pl.pallas_call(
        functools.partial(_decode_kernel, npg=npg, nt=nt, nh=nh, lam_init=lam_init),
        grid_spec=grid_spec,
        out_shape=jax.ShapeDtypeStruct((nb, nt, aw), F32),
        compiler_params=pltpu.CompilerParams(dimension_semantics=("arbitrary", "arbitrary"),
                                             vmem_limit_bytes=VMEM_LIMIT),
        name="sample_attention",
    )(pt_flat, q_rep, k_new, v_new, lamqk, g_sub, *([cache_kt] * npg), *([cache_v] * npg))


def _post_kernel(h_ref, at_ref, co_ref, wo_ref, gf_ref, wup_ref, fcw_ref, prev_ref, wd_ref,
                 p_ref, wple_ref, wpg_ref, gfin_ref, out_ref, st_ref, gbuf,
                 *, tm, pad, stride, dff, chunk, final):
    i = pl.program_id(0)
    cat = jnp.concatenate([at_ref[...], co_ref[...]], axis=1)
    h1 = h_ref[...] + jnp.dot(cat, wo_ref[...], preferred_element_type=F32)
    f = _rms(h1, gf_ref[...]).astype(BF16)

    @pl.when(i == 0)
    def _():
        gbuf[0:pad, :] = prev_ref[...]

    acc = h1
    for lo in range(0, dff, chunk):
        cols = slice(lo, lo + chunk)
        gate = jnp.dot(f, wup_ref[:, lo:lo + chunk], preferred_element_type=F32)
        up = jnp.dot(f, wup_ref[:, dff + lo:dff + lo + chunk], preferred_element_type=F32)
        gbuf[pad:pad + tm, cols] = gate
        gate_c = _causal_conv3(gbuf, cols, fcw_ref[:, cols], gate, pad, stride, tm)
        act = (gate_c * jax.nn.sigmoid(gate_c) * up).astype(BF16)
        acc = acc + jnp.dot(act, wd_ref[lo:lo + chunk, :], preferred_element_type=F32)

    tail = gbuf[tm:tm + pad, :]
    st_ref[...] = tail
    gbuf[0:pad, :] = tail

    h2 = acc
    ple = jnp.dot(p_ref[...].astype(BF16), wple_ref[...], preferred_element_type=F32)
    pg = jnp.dot(h2.astype(BF16), wpg_ref[...], preferred_element_type=F32)
    h3 = h2 + ple * jax.nn.sigmoid(pg)
    out_ref[...] = _rms(h3, gfin_ref[...]) if final else h3


def _post(h, attn, conv, wo_bf, g_ffn, wup_bf, ffn_conv_w, prev, wd_bf, p, wple_bf, wpg_bf, g_final,
          *, layer, tm, stride, chunk, final):
    r, d = h.shape
    dff = wd_bf.shape[1]
    pad = prev.shape[0]
    row = lambda width: pl.BlockSpec((tm, width), lambda i: (i, 0))
    slab = lambda x: _resident(x.shape[1:], layer)
    kern = functools.partial(_post_kernel, tm=tm, pad=pad, stride=stride, dff=dff, chunk=chunk,
                             final=final)
    return pl.pallas_call(
        kern,
        grid=(r // tm,),
        in_specs=[row(d), row(attn.shape[1]), row(conv.shape[1]), slab(wo_bf),
                  slab(g_ffn), slab(wup_bf), slab(ffn_conv_w),
                  _resident(prev.shape), slab(wd_bf),
                  pl.BlockSpec((None, tm, p.shape[2]), lambda i: (layer, i, 0)),
                  slab(wple_bf), slab(wpg_bf), _resident((1, d))],
        out_specs=[row(d), pl.BlockSpec((pad, dff), lambda i: (0, 0))],
        out_shape=[jax.ShapeDtypeStruct((r, d), F32),
                   jax.ShapeDtypeStruct((pad, dff), F32)],
        scratch_shapes=[pltpu.VMEM((pad + tm, dff), F32)],
        compiler_params=pltpu.CompilerParams(dimension_semantics=("arbitrary",),
                                             vmem_limit_bytes=VMEM_LIMIT),
        name="post",
    )(h, attn, conv, wo_bf, g_ffn, wup_bf, ffn_conv_w, prev, wd_bf, p, wple_bf, wpg_bf, g_final)


def _rope_tables(pos):
    half = ROT_DIM // 2
    inv = ROPE_THETA ** (-jnp.arange(0, ROT_DIM, 2, dtype=F32) / ROT_DIM)
    ang = pos.astype(F32)[:, None] * inv[None, :]
    cos, sin = jnp.cos(ang), jnp.sin(ang)
    ones = jnp.ones((pos.shape[0], HEAD_DIM - ROT_DIM), F32)
    zeros = jnp.zeros((pos.shape[0], HEAD_DIM - ROT_DIM), F32)
    zh = jnp.zeros_like(sin)
    tile = lambda parts: jnp.tile(jnp.concatenate(parts, axis=1), (1, LANES // HEAD_DIM))
    return tile([cos, cos, ones]), tile([-sin, zh, zeros]), tile([zh, sin, zeros])


def _pad_front(x, pad):
    return jnp.pad(x, ((pad - x.shape[0], 0), (0, 0)))


def _pick_tile(rows, want):
    return want if rows % want == 0 else rows


def kernel(x_prompt, x_sample, cache_k, cache_v, state_conv, state_ffn_conv, page_table,
           p_prompt, p_sample, g_attn, w_in, lambda_qk, g_subln, conv_w, w_o, g_ffn, w_up,
           ffn_conv_w, w_down, w_ple, w_pg, g_final):
    depth = w_in.shape[0]
    b, s, d = x_prompt.shape
    assert b == 1
    db, t, _ = x_sample.shape
    nh = cache_k.shape[3]
    aw = nh * V_DIM
    cwid = conv_w.shape[2]
    dff = w_down.shape[1]
    page = cache_k.shape[2]
    past_len = page_table.shape[1] * page
    n_phys = cache_k.shape[1]
    ckt = cache_k.transpose(0, 1, 3, 4, 5, 2).reshape(depth, n_phys, aw, page)
    cv2 = cache_v.reshape(depth, n_phys, page * nh, V_DIM)

    tabs_p = _rope_tables(jnp.arange(s))
    tabs_s = _rope_tables(past_len + jnp.repeat(jnp.arange(t), db))
    pad_p = SUBLANES
    pad_s = _round_up(2 * db, SUBLANES)

    hp = x_prompt.reshape(s, d)
    hs = x_sample.transpose(1, 0, 2).reshape(t * db, d)
    to_tm = lambda x: x.transpose(1, 0, 2).reshape(x.shape[0] * x.shape[1], x.shape[2])
    from_tm = lambda x, rows: x.reshape(rows, db, x.shape[1]).transpose(1, 0, 2)

    tm_post = _pick_tile(s, 256)
    tq = _pick_tile(s, 512)
    chunk = dff
    npg = next(n for n in (16, 8, 4, 2, 1) if page_table.shape[1] % n == 0)

    win_bf, wo_bf, wup_bf = w_in.astype(BF16), w_o.astype(BF16), w_up.astype(BF16)
    wd_bf, wple_bf, wpg_bf = w_down.astype(BF16), w_ple.astype(BF16), w_pg.astype(BF16)
    ga, gf, gfin = g_attn[:, None, :], g_ffn[:, None, :], g_final[None]
    pp = p_prompt.reshape(depth, s, -1)
    ps = p_sample.transpose(0, 2, 1, 3).reshape(depth, t * db, -1)

    kp, vp, cp, fp, ksm, vsm, csm, fsm = [], [], [], [], [], [], [], []
    for i in range(depth):
        lam_init = 0.8 - 0.6 * math.exp(-0.3 * i)
        final = i == depth - 1
        post_args = (wo_bf, gf, wup_bf, ffn_conv_w)

        q_t, k_t, kb, v_rows, v_t, co, cst = _in_proj(
            hp, ga, win_bf, tabs_p, conv_w, jnp.zeros((pad_p, cwid), F32),
            layer=i, tm=tq, stride=1, prompt=True)
        attn = _prompt_attention(q_t, kb, v_t, lambda_qk[i], g_subln[i][:, None], lam_init=lam_init)
        hp, fst = _post(hp, attn, co, *post_args, jnp.zeros((pad_p, dff), F32), wd_bf, pp,
                        wple_bf, wpg_bf, gfin,
                        layer=i, tm=tm_post, stride=1, chunk=chunk, final=final)
        kp.append(k_t)
        vp.append(v_rows)
        cp.append(cst[pad_p - 2:])
        fp.append(fst[pad_p - 2:])

        q, k, v, co, cst = _in_proj(
            hs, ga, win_bf, tabs_s, conv_w, _pad_front(to_tm(state_conv[i]), pad_s),
            layer=i, tm=t * db, stride=db, prompt=False)
        q_rep = jnp.tile(from_tm(q, t).astype(F32), (1, 2 * nh, 1))
        k_new, v_new = from_tm(k, t), from_tm(v, t)
        attn = _sample_attention(page_table, q_rep, k_new, v_new, lambda_qk[i], g_subln[i][None],
                                 ckt, cv2, layer=i, npg=npg, lam_init=lam_init)
        hs, fst = _post(hs, to_tm(attn).astype(BF16), co, *post_args,
                        _pad_front(to_tm(state_ffn_conv[i]), pad_s), wd_bf, ps,
                        wple_bf, wpg_bf, gfin,
                        layer=i, tm=t * db, stride=db, chunk=chunk, final=final)
        ksm.append(k_new)
        vsm.append(v_new)
        csm.append(from_tm(cst[pad_s - 2 * db:], 2))
        fsm.append(from_tm(fst[pad_s - 2 * db:], 2))

    y_prompt = hp.reshape(b, s, d)
    y_sample = from_tm(hs, t)
    k_prompt = jnp.stack(kp).reshape(depth, b, nh, 2, HEAD_DIM, s).transpose(0, 1, 5, 2, 3, 4)
    return (y_prompt, y_sample,
            k_prompt,
            jnp.stack(vp).reshape(depth, b, s, nh, V_DIM),
            jnp.stack(cp).reshape(depth, b, 2, cwid),
            jnp.stack(fp).reshape(depth, b, 2, dff),
            jnp.stack(ksm).reshape(depth, db, t, nh, 2, HEAD_DIM),
            jnp.stack(vsm).reshape(depth, db, t, nh, V_DIM),
            jnp.stack(csm), jnp.stack(fsm))
```

```python
import functools
import math

import jax
import jax.numpy as jnp
from jax import lax
from jax.experimental import pallas as pl
from jax.experimental.pallas import tpu as pltpu

HEAD_DIM = 64
HEAD_SHIFT = 6
V_DIM = 2 * HEAD_DIM
ROT_DIM = HEAD_DIM // 4
ROPE_THETA = 500000.0
CONV_K = 3
RMS_EPS = 1e-6
QK_SCALE = HEAD_DIM ** -0.5 * math.log2(math.e)
FLASH_ROWS = 16
SUM_ROWS = 16
FLASH_UNROLL = 2
LANES = 128
SUBLANES = 8
VMEM_LIMIT = 56 * 1024 * 1024

F32 = jnp.float32
BF16 = jnp.bfloat16
NT_DIMS = (((1,), (1,)), ((), ()))


def _round_up(x, m):
    return (x + m - 1) // m * m


def _rms(x, g):
    return x * lax.rsqrt(jnp.mean(x * x, axis=-1, keepdims=True) + RMS_EPS) * g


def _resident(shape, layer=None):
    if layer is None:
        return pl.BlockSpec(shape, lambda *_: (0,) * len(shape), pipeline_mode=pl.Buffered(1))
    return pl.BlockSpec((None,) + tuple(shape), lambda *_: (layer,) + (0,) * len(shape),
                        pipeline_mode=pl.Buffered(1))


def _diff_lambda(lamqk, lam_init):
    a = jnp.sum(lamqk[0:1] * lamqk[1:2], axis=1, keepdims=True)
    b = jnp.sum(lamqk[2:3] * lamqk[3:4], axis=1, keepdims=True)
    return jnp.exp(a) - jnp.exp(b) + lam_init


def _causal_conv3(buf_ref, cols, w, x, pad, stride, tm):
    x2 = buf_ref[pad - 2 * stride:pad - 2 * stride + tm, cols]
    x1 = buf_ref[pad - stride:pad - stride + tm, cols]
    return w[0:1] * x2 + w[1:2] * x1 + w[2:3] * x


def _in_proj_kernel(h_ref, g_ref, w_ref, cos_ref, sa_ref, sb_ref, cw_ref, prev_ref, *rest,
                    tm, pad, stride, aw, cwid, prompt):
    if prompt:
        qt_ref, k_ref, kb_ref, v_ref, vt_ref, co_ref, st_ref, cbuf = rest
    else:
        q_ref, k_ref, v_ref, co_ref, st_ref, cbuf = rest
    i = pl.program_id(0)
    a = _rms(h_ref[...], g_ref[...]).astype(BF16)

    def proj(lo, width):
        return jnp.dot(a, w_ref[:, lo:lo + width], preferred_element_type=F32)

    cos, sa, sb = cos_ref[...], sa_ref[...], sb_ref[...]
    half = ROT_DIM // 2

    def rope(x):
        outs = []
        for hh in range(aw // LANES):
            xs = x[:, hh * LANES:(hh + 1) * LANES]
            up = pltpu.roll(xs, LANES - half, axis=1)
            dn = pltpu.roll(xs, half, axis=1)
            outs.append(xs * cos + up * sa + dn * sb)
        return jnp.concatenate(outs, axis=1)

    nh = aw // V_DIM
    q = rope(proj(0, aw)) * QK_SCALE
    k = rope(proj(aw, aw))
    v = proj(2 * aw, aw)
    if prompt:
        qt_ref[...] = q.T.astype(BF16)
        k_ref[...] = k.T
        kb_ref[...] = k.astype(BF16)
        for hh in range(nh):
            v_ref[pl.ds(hh, tm, stride=nh), :] = v[:, hh * V_DIM:(hh + 1) * V_DIM]
        vt_ref[:, 0, 0:V_DIM, :] = v.T.reshape(nh, V_DIM, tm).astype(BF16)
        vt_ref[:, 0, V_DIM:, :] = jnp.ones((nh, SUM_ROWS, tm), BF16)
    else:
        q_ref[...] = q.astype(BF16)
        k_ref[...] = k
        v_ref[...] = v

    gb = proj(3 * aw, cwid)
    cu = proj(3 * aw + cwid, cwid) * proj(3 * aw + 2 * cwid, cwid)

    @pl.when(i == 0)
    def _():
        cbuf[0:pad, :] = prev_ref[...]

    cbuf[pad:pad + tm, :] = cu
    y = _causal_conv3(cbuf, slice(None), cw_ref[...], cu, pad, stride, tm)
    co_ref[...] = (gb * y).astype(BF16)
    tail = cbuf[tm:tm + pad, :]
    st_ref[...] = tail
    cbuf[0:pad, :] = tail


def _in_proj(h, g, w_bf, tabs, conv_w, prev, *, layer, tm, stride, prompt):
    r, d = h.shape
    aw = cwid = conv_w.shape[2]
    nh = aw // V_DIM
    pad = prev.shape[0]
    cos, sa, sb = tabs
    row = lambda width: pl.BlockSpec((tm, width), lambda i: (i, 0))
    kern = functools.partial(_in_proj_kernel, tm=tm, pad=pad, stride=stride, aw=aw, cwid=cwid,
                             prompt=prompt)
    f32_rows = (row(aw), jax.ShapeDtypeStruct((r, aw), F32))
    bf_rows = (row(aw), jax.ShapeDtypeStruct((r, aw), BF16))
    conv_out = (row(cwid), jax.ShapeDtypeStruct((r, cwid), BF16))
    state = (pl.BlockSpec((pad, cwid), lambda i: (0, 0)),
             jax.ShapeDtypeStruct((pad, cwid), F32))
    if prompt:
        q_t = (pl.BlockSpec((aw, tm), lambda i: (0, i)), jax.ShapeDtypeStruct((aw, r), BF16))
        v_t = (pl.BlockSpec((nh, 1, V_DIM + SUM_ROWS, tm), lambda i: (0, i, 0, 0)),
               jax.ShapeDtypeStruct((nh, r // tm, V_DIM + SUM_ROWS, tm), BF16))
        k_t = (pl.BlockSpec((aw, tm), lambda i: (0, i)), jax.ShapeDtypeStruct((aw, r), F32))
        v_rows = (pl.BlockSpec((tm * nh, V_DIM), lambda i: (i, 0)),
                  jax.ShapeDtypeStruct((r * nh, V_DIM), F32))
        outs = [q_t, k_t, bf_rows, v_rows, v_t, conv_out, state]
    else:
        outs = [bf_rows, f32_rows, f32_rows, conv_out, state]
    return pl.pallas_call(
        kern,
        grid=(r // tm,),
        in_specs=[row(d), _resident((1, d), layer), _resident(w_bf.shape[1:], layer),
                  row(LANES), row(LANES), row(LANES),
                  _resident(conv_w.shape[1:], layer), _resident(prev.shape)],
        out_specs=[o[0] for o in outs],
        out_shape=[o[1] for o in outs],
        scratch_shapes=[pltpu.VMEM((pad + tm, cwid), F32)],
        compiler_params=pltpu.CompilerParams(dimension_semantics=("arbitrary",),
                                             vmem_limit_bytes=VMEM_LIMIT),
        name="in_proj",
    )(h, g, w_bf, cos, sa, sb, conv_w, prev)


def _flash_kernel(qt_ref, k_ref, vt_ref, lam_ref, g_ref, o_ref,
                  qz_ref, sa_ref, sb_ref, pa_ref, pb_ref, aa_ref, ab_ref, m_ref, acc_ref,
                  *, tq, lam_init):
    i = pl.program_id(1)
    qt = qt_ref[...]
    row = lax.broadcasted_iota(jnp.int32, qt.shape, 0)
    zero = jnp.zeros_like(qt)
    qz_ref[:, 0:tq] = jnp.where(row < HEAD_DIM, qt, zero)
    qz_ref[:, tq:2 * tq] = jnp.where(row >= HEAD_DIM, qt, zero)
    m_ref[...] = jnp.full(m_ref.shape, -jnp.inf, F32)
    acc_ref[...] = jnp.zeros(acc_ref.shape, F32)

    def softmax_cols(s_ref, p_ref, a_ref, c0, masked):
        cols = slice(c0, c0 + LANES)
        shape = (FLASH_ROWS, LANES)

        q_lo = c0 % tq
        live = min(tq, q_lo + LANES) if masked else tq

        def scores(r0):
            s = s_ref[r0:r0 + FLASH_ROWS, cols]
            if masked and r0 + FLASH_ROWS - 1 > q_lo:
                ki = r0 + lax.broadcasted_iota(jnp.int32, shape, 0)
                qi = q_lo + lax.broadcasted_iota(jnp.int32, shape, 1)
                s = jnp.where(ki <= qi, s, -jnp.inf)
            return s

        mx = scores(0)
        for r0 in range(FLASH_ROWS, live, FLASH_ROWS):
            mx = jnp.maximum(mx, scores(r0))
        m_old = m_ref[:, cols]
        m_new = jnp.maximum(m_old, jnp.max(mx, axis=0, keepdims=True))
        for r0 in range(0, live, FLASH_ROWS):
            p_ref[r0:r0 + FLASH_ROWS, cols] = jnp.exp2(scores(r0) - m_new).astype(BF16)
        if live < tq:
            p_ref[live:tq, cols] = jnp.zeros((tq - live, LANES), BF16)
        m_ref[:, cols] = m_new
        a_ref[:, cols] = jnp.exp2(m_old - m_new)

    def scores_into(s_ref, j):
        start = pl.multiple_of(j * tq, tq)
        s_ref[...] = jnp.dot(k_ref[pl.ds(start, tq), :], qz_ref[...],
                             preferred_element_type=F32)

    def softmax(bufs, masked):
        for c0 in range(0, 2 * tq, LANES):
            softmax_cols(*bufs, c0, masked)

    def accumulate(bufs, j):
        _, p_ref, a_ref = bufs
        acc_ref[...] = a_ref[...] * acc_ref[...] + jnp.dot(
            vt_ref[jnp.maximum(j, 0)], p_ref[...], preferred_element_type=F32)

    buf_a = (sa_ref, pa_ref, aa_ref)
    buf_b = (sb_ref, pb_ref, ab_ref)
    pb_ref[...] = jnp.zeros(pb_ref.shape, BF16)
    ab_ref[...] = jnp.zeros(ab_ref.shape, F32)
    scores_into(sa_ref, 0)

    def run_tiles(first, count, diagonal_last):
        cur, other = buf_a, buf_b
        for n in range(count):
            j = first + n
            last = n == count - 1
            if not (diagonal_last and last):
                scores_into(other[0], j + 1)
            accumulate(other, j - 1)
            softmax(cur, diagonal_last and last)
            cur, other = other, cur
        return cur, other

    def unrolled(jj, carry):
        run_tiles(FLASH_UNROLL * jj, FLASH_UNROLL, False)
        return carry

    lax.fori_loop(0, i // FLASH_UNROLL, unrolled, 0)

    for rest in range(FLASH_UNROLL):
        @pl.when(i % FLASH_UNROLL == rest)
        def _(rest=rest):
            _, done = run_tiles(i - rest, rest + 1, True)
            accumulate(done, i)

    o = acc_ref[0:V_DIM, :] / acc_ref[V_DIM:V_DIM + 1, :]
    lam = _diff_lambda(lam_ref[...], lam_init)
    o = o[:, 0:tq] - lam * o[:, tq:2 * tq]
    o = o * lax.rsqrt(jnp.mean(o * o, axis=0, keepdims=True) + RMS_EPS) * g_ref[...]
    o_ref[...] = (o * (1.0 - lam_init)).T.astype(o_ref.dtype)


def _prompt_attention(q_t, k, v_t, lamqk, g_col, *, lam_init):
    aw, t = q_t.shape
    nh, nkv, vrows, tq = v_t.shape
    return pl.pallas_call(
        functools.partial(_flash_kernel, tq=tq, lam_init=lam_init),
        grid=(nh, t // tq),
        in_specs=[pl.BlockSpec((V_DIM, tq), lambda h, i: (h, i)),
                  pl.BlockSpec((t, V_DIM), lambda h, i: (0, h)),
                  pl.BlockSpec((None, nkv, vrows, tq), lambda h, i: (h, 0, 0, 0)),
                  _resident(lamqk.shape), _resident(g_col.shape)],
        out_specs=pl.BlockSpec((tq, V_DIM), lambda h, i: (i, h)),
        out_shape=jax.ShapeDtypeStruct((t, aw), BF16),
        scratch_shapes=[pltpu.VMEM((V_DIM, 2 * tq), BF16),
                        pltpu.VMEM((tq, 2 * tq), F32),
                        pltpu.VMEM((tq, 2 * tq), F32),
                        pltpu.VMEM((tq, 2 * tq), BF16),
                        pltpu.VMEM((tq, 2 * tq), BF16),
                        pltpu.VMEM((1, 2 * tq), F32),
                        pltpu.VMEM((1, 2 * tq), F32),
                        pltpu.VMEM((1, 2 * tq), F32),
                        pltpu.VMEM((vrows, 2 * tq), F32)],
        compiler_params=pltpu.CompilerParams(dimension_semantics=("arbitrary", "arbitrary"),
                                             vmem_limit_bytes=VMEM_LIMIT),
        name="prompt_attention",
    )(q_t, k, v_t, lamqk, g_col)


def _decode_kernel(pt_ref, q_ref, kn_ref, vn_ref, lam_ref, g_ref, *rest, npg, nt, nh, lam_init):
    kt_refs = rest[:npg]
    v_refs = rest[npg:2 * npg]
    o_ref, qbd_ref, m_ref, l_ref, acc_ref = rest[2 * npg:]
    c = pl.program_id(1)
    nrow, aw = qbd_ref.shape
    hrows = nrow // nh

    @pl.when(c == 0)
    def _():
        qrep = q_ref[0]
        r = lax.broadcasted_iota(jnp.int32, (nrow, aw), 0)
        lo = (lax.broadcasted_iota(jnp.int32, (nrow, aw), 1) >> HEAD_SHIFT) * nt
        qbd_ref[...] = jnp.where((r >= lo) & (r < lo + nt), qrep, 0.0).astype(BF16)
        m_ref[...] = jnp.full(m_ref.shape, -jnp.inf, F32)
        l_ref[...] = jnp.zeros(l_ref.shape, F32)
        acc_ref[...] = jnp.zeros(acc_ref.shape, F32)

    qbd = qbd_ref[...]
    s = jnp.concatenate(
        [jnp.dot(qbd, kt[...].astype(BF16), preferred_element_type=F32) for kt in kt_refs],
        axis=1)
    m_prev = m_ref[...]
    m_new = jnp.maximum(m_prev, jnp.max(s, axis=1, keepdims=True))
    alpha = jnp.exp2(m_prev - m_new)
    p = jnp.exp2(s - m_new)
    l_ref[...] = alpha * l_ref[...] + jnp.sum(p, axis=1, keepdims=True)
    page = s.shape[1] // npg
    pv = []
    for hh in range(nh):
        ph = p[hh * hrows:(hh + 1) * hrows].astype(BF16)
        pvh = jnp.zeros((hrows, V_DIM), F32)
        for j in range(npg):
            vh = v_refs[j][pl.ds(hh, page, stride=nh), :]
            pvh += jnp.dot(ph[:, j * page:(j + 1) * page], vh.astype(BF16),
                           preferred_element_type=F32)
        pv.append(pvh)
    acc_ref[...] = alpha * acc_ref[...] + jnp.concatenate(pv, axis=0)
    m_ref[...] = m_new

    @pl.when(c == pl.num_programs(1) - 1)
    def _():
        qf = qbd.astype(F32)
        kn, vn = kn_ref[0], vn_ref[0]
        tq = lax.broadcasted_iota(jnp.int32, (nrow, 1), 0) & (nt - 1)
        s_new = []
        for tk in range(nt):
            st = jnp.sum(qf * kn[tk:tk + 1, :], axis=1, keepdims=True)
            s_new.append(jnp.where(tk <= tq, st, -jnp.inf))
        m_fin = m_new
        for st in s_new:
            m_fin = jnp.maximum(m_fin, st)
        a_fin = jnp.exp2(m_new - m_fin)
        l = a_fin * l_ref[...]
        acc = a_fin * acc_ref[...]
        for tk in range(nt):
            pn = jnp.exp2(s_new[tk] - m_fin)
            l = l + pn
            v_rows = jnp.concatenate(
                [jnp.broadcast_to(vn[tk:tk + 1, hh * V_DIM:(hh + 1) * V_DIM], (hrows, V_DIM))
                 for hh in range(nh)], axis=0)
            acc = acc + pn * v_rows
        acc_ref[...] = acc / l
        lam = _diff_lambda(lam_ref[...], lam_init)
        g = g_ref[...]
        for hh in range(nh):
            r0 = hrows * hh
            o = acc_ref[r0:r0 + nt, :] - lam * acc_ref[r0 + nt:r0 + 2 * nt, :]
            o_ref[0, :, hh * V_DIM:(hh + 1) * V_DIM] = _rms(o, g) * (1.0 - lam_init)


def _sample_attention(page_table, q_rep, k_new, v_new, lamqk, g_sub, cache_kt, cache_v,
                      *, layer, npg, lam_init):
    nb, nt, aw = k_new.shape
    n_pages = page_table.shape[1]
    page = cache_kt.shape[3]
    nh = aw // V_DIM
    nrow = q_rep.shape[1]
    pt_flat = page_table.reshape(-1)

    def phys(b, c, pt, j):
        return pt[b * n_pages + c * npg + j]

    def kt_spec(j):
        return pl.BlockSpec((None, None, aw, page), lambda b, c, pt: (layer, phys(b, c, pt, j), 0, 0))

    def v_spec(j):
        return pl.BlockSpec((None, None, page * nh, V_DIM),
                            lambda b, c, pt: (layer, phys(b, c, pt, j), 0, 0))

    per_seq = lambda rows: pl.BlockSpec((1, rows, aw), lambda b, c, pt: (b, 0, 0))
    const = lambda shape: pl.BlockSpec(shape, lambda b, c, pt: (0,) * len(shape))
    grid_spec = pltpu.PrefetchScalarGridSpec(
        num_scalar_prefetch=1,
        grid=(nb, n_pages // npg),
        in_specs=[per_seq(nrow), per_seq(nt), per_seq(nt), const(lamqk.shape), const(g_sub.shape)]
                 + [kt_spec(j) for j in range(npg)] + [v_spec(j) for j in range(npg)],
        out_specs=per_seq(nt),
        scratch_shapes=[pltpu.VMEM((nrow, aw), BF16), pltpu.VMEM((nrow, 1), F32),
                        pltpu.VMEM((nrow, 1), F32), pltpu.VMEM((nrow, V_DIM), F32)],
    )
    return pl.pallas_call(
        functools.partial(_decode_kernel, npg=npg, nt=nt, nh=nh, lam_init=lam_init),
        grid_spec=grid_spec,
        out_shape=jax.ShapeDtypeStruct((nb, nt, aw), F32),
        compiler_params=pltpu.CompilerParams(dimension_semantics=("arbitrary", "arbitrary"),
                                             vmem_limit_bytes=VMEM_LIMIT),
        name="sample_attention",
    )(pt_flat, q_rep, k_new, v_new, lamqk, g_sub, *([cache_kt] * npg), *([cache_v] * npg))


def _post_kernel(h_ref, at_ref, co_ref, wo_ref, gf_ref, wup_ref, fcw_ref, prev_ref, wd_ref,
                 p_ref, wple_ref, wpg_ref, gfin_ref, out_ref, st_ref, gbuf,
                 *, tm, pad, stride, dff, chunk, final):
    i = pl.program_id(0)
    cat = jnp.concatenate([at_ref[...], co_ref[...]], axis=1)
    h1 = h_ref[...] + jnp.dot(cat, wo_ref[...], preferred_element_type=F32)
    f = _rms(h1, gf_ref[...]).astype(BF16)

    @pl.when(i == 0)
    def _():
        gbuf[0:pad, :] = prev_ref[...]

    acc = h1
    for lo in range(0, dff, chunk):
        cols = slice(lo, lo + chunk)
        gate = jnp.dot(f, wup_ref[:, lo:lo + chunk], preferred_element_type=F32)
        up = jnp.dot(f, wup_ref[:, dff + lo:dff + lo + chunk], preferred_element_type=F32)
        gbuf[pad:pad + tm, cols] = gate
        gate_c = _causal_conv3(gbuf, cols, fcw_ref[:, cols], gate, pad, stride, tm)
        act = (gate_c * jax.nn.sigmoid(gate_c) * up).astype(BF16)
        acc = acc + jnp.dot(act, wd_ref[lo:lo + chunk, :], preferred_element_type=F32)

    tail = gbuf[tm:tm + pad, :]
    st_ref[...] = tail
    gbuf[0:pad, :] = tail

    h2 = acc
    ple = jnp.dot(p_ref[...].astype(BF16), wple_ref[...], preferred_element_type=F32)
    pg = jnp.dot(h2.astype(BF16), wpg_ref[...], preferred_element_type=F32)
    h3 = h2 + ple * jax.nn.sigmoid(pg)
    out_ref[...] = _rms(h3, gfin_ref[...]) if final else h3


def _post(h, attn, conv, wo_bf, g_ffn, wup_bf, ffn_conv_w, prev, wd_bf, p, wple_bf, wpg_bf, g_final,
          *, layer, tm, stride, chunk, final):
    r, d = h.shape
    dff = wd_bf.shape[1]
    pad = prev.shape[0]
    row = lambda width: pl.BlockSpec((tm, width), lambda i: (i, 0))
    slab = lambda x: _resident(x.shape[1:], layer)
    kern = functools.partial(_post_kernel, tm=tm, pad=pad, stride=stride, dff=dff, chunk=chunk,
                             final=final)
    return pl.pallas_call(
        kern,
        grid=(r // tm,),
        in_specs=[row(d), row(attn.shape[1]), row(conv.shape[1]), slab(wo_bf),
                  slab(g_ffn), slab(wup_bf), slab(ffn_conv_w),
                  _resident(prev.shape), slab(wd_bf),
                  pl.BlockSpec((None, tm, p.shape[2]), lambda i: (layer, i, 0)),
                  slab(wple_bf), slab(wpg_bf), _resident((1, d))],
        out_specs=[row(d), pl.BlockSpec((pad, dff), lambda i: (0, 0))],
        out_shape=[jax.ShapeDtypeStruct((r, d), F32),
                   jax.ShapeDtypeStruct((pad, dff), F32)],
        scratch_shapes=[pltpu.VMEM((pad + tm, dff), F32)],
        compiler_params=pltpu.CompilerParams(dimension_semantics=("arbitrary",),
                                             vmem_limit_bytes=VMEM_LIMIT),
        name="post",
    )(h, attn, conv, wo_bf, g_ffn, wup_bf, ffn_conv_w, prev, wd_bf, p, wple_bf, wpg_bf, g_final)


def _rope_tables(pos):
    half = ROT_DIM // 2
    inv = ROPE_THETA ** (-jnp.arange(0, ROT_DIM, 2, dtype=F32) / ROT_DIM)
    ang = pos.astype(F32)[:, None] * inv[None, :]
    cos, sin = jnp.cos(ang), jnp.sin(ang)
    ones = jnp.ones((pos.shape[0], HEAD_DIM - ROT_DIM), F32)
    zeros = jnp.zeros((pos.shape[0], HEAD_DIM - ROT_DIM), F32)
    zh = jnp.zeros_like(sin)
    tile = lambda parts: jnp.tile(jnp.concatenate(parts, axis=1), (1, LANES // HEAD_DIM))
    return tile([cos, cos, ones]), tile([-sin, zh, zeros]), tile([zh, sin, zeros])


def _pad_front(x, pad):
    return jnp.pad(x, ((pad - x.shape[0], 0), (0, 0)))


def _pick_tile(rows, want):
    return want if rows % want == 0 else rows


def kernel(x_prompt, x_sample, cache_k, cache_v, state_conv, state_ffn_conv, page_table,
           p_prompt, p_sample, g_attn, w_in, lambda_qk, g_subln, conv_w, w_o, g_ffn, w_up,
           ffn_conv_w, w_down, w_ple, w_pg, g_final):
    depth = w_in.shape[0]
    b, s, d = x_prompt.shape
    assert b == 1
    db, t, _ = x_sample.shape
    nh = cache_k.shape[3]
    aw = nh * V_DIM
    cwid = conv_w.shape[2]
    dff = w_down.shape[1]
    page = cache_k.shape[2]
    past_len = page_table.shape[1] * page
    n_phys = cache_k.shape[1]
    ckt = cache_k.transpose(0, 1, 3, 4, 5, 2).reshape(depth, n_phys, aw, page)
    cv2 = cache_v.reshape(depth, n_phys, page * nh, V_DIM)

    tabs_p = _rope_tables(jnp.arange(s))
    tabs_s = _rope_tables(past_len + jnp.repeat(jnp.arange(t), db))
    pad_p = SUBLANES
    pad_s = _round_up(2 * db, SUBLANES)

    hp = x_prompt.reshape(s, d)
    hs = x_sample.transpose(1, 0, 2).reshape(t * db, d)
    to_tm = lambda x: x.transpose(1, 0, 2).reshape(x.shape[0] * x.shape[1], x.shape[2])
    from_tm = lambda x, rows: x.reshape(rows, db, x.shape[1]).transpose(1, 0, 2)

    tm_post = _pick_tile(s, 256)
    tq = _pick_tile(s, 512)
    chunk = dff
    npg = next(n for n in (32, 16, 8, 4, 2, 1) if page_table.shape[1] % n == 0)

    win_bf, wo_bf, wup_bf = w_in.astype(BF16), w_o.astype(BF16), w_up.astype(BF16)
    wd_bf, wple_bf, wpg_bf = w_down.astype(BF16), w_ple.astype(BF16), w_pg.astype(BF16)
    ga, gf, gfin = g_attn[:, None, :], g_ffn[:, None, :], g_final[None]
    pp = p_prompt.reshape(depth, s, -1)
    ps = p_sample.transpose(0, 2, 1, 3).reshape(depth, t * db, -1)

    kp, vp, cp, fp, ksm, vsm, csm, fsm = [], [], [], [], [], [], [], []
    for i in range(depth):
        lam_init = 0.8 - 0.6 * math.exp(-0.3 * i)
        final = i == depth - 1
        post_args = (wo_bf, gf, wup_bf, ffn_conv_w)

        q_t, k_t, kb, v_rows, v_t, co, cst = _in_proj(
            hp, ga, win_bf, tabs_p, conv_w, jnp.zeros((pad_p, cwid), F32),
            layer=i, tm=tq, stride=1, prompt=True)
        attn = _prompt_attention(q_t, kb, v_t, lambda_qk[i], g_subln[i][:, None], lam_init=lam_init)
        hp, fst = _post(hp, attn, co, *post_args, jnp.zeros((pad_p, dff), F32), wd_bf, pp,
                        wple_bf, wpg_bf, gfin,
                        layer=i, tm=tm_post, stride=1, chunk=chunk, final=final)
        kp.append(k_t)
        vp.append(v_rows)
        cp.append(cst[pad_p - 2:])
        fp.append(fst[pad_p - 2:])

        q, k, v, co, cst = _in_proj(
            hs, ga, win_bf, tabs_s, conv_w, _pad_front(to_tm(state_conv[i]), pad_s),
            layer=i, tm=t * db, stride=db, prompt=False)
        q_rep = jnp.tile(from_tm(q, t).astype(F32), (1, 2 * nh, 1))
        k_new, v_new = from_tm(k, t), from_tm(v, t)
        attn = _sample_attention(page_table, q_rep, k_new, v_new, lambda_qk[i], g_subln[i][None],
                                 ckt, cv2, layer=i, npg=npg, lam_init=lam_init)
        hs, fst = _post(hs, to_tm(attn).astype(BF16), co, *post_args,
                        _pad_front(to_tm(state_ffn_conv[i]), pad_s), wd_bf, ps,
                        wple_bf, wpg_bf, gfin,
                        layer=i, tm=t * db, stride=db, chunk=chunk, final=final)
        ksm.append(k_new)
        vsm.append(v_new)
        csm.append(from_tm(cst[pad_s - 2 * db:], 2))
        fsm.append(from_tm(fst[pad_s - 2 * db:], 2))

    y_prompt = hp.reshape(b, s, d)
    y_sample = from_tm(hs, t)
    k_prompt = jnp.stack(kp).reshape(depth, b, nh, 2, HEAD_DIM, s).transpose(0, 1, 5, 2, 3, 4)
    return (y_prompt, y_sample,
            k_prompt,
            jnp.stack(vp).reshape(depth, b, s, nh, V_DIM),
            jnp.stack(cp).reshape(depth, b, 2, cwid),
            jnp.stack(fp).reshape(depth, b, 2, dff),
            jnp.stack(ksm).reshape(depth, db, t, nh, 2, HEAD_DIM),
            jnp.stack(vsm).reshape(depth, db, t, nh, V_DIM),
            jnp.stack(csm), jnp.stack(fsm))
```
